```python
import jax, jax.numpy as jnp
from jax import lax
import numpy as np

D_MODEL = 1024
BATCH = 4
SEQ = 4096
DEPTH = 1
DEC_BATCH = 8
DEC_SEQ = 8192
PAST_LEN = 128

N_MEM = 256
GRID_W = 64
HEAD_DIM = 64
D_MIX = D_MODEL
D_SGU = D_MIX // 2
SGU_HEADS = D_SGU // HEAD_DIM
D_NA = D_MIX - D_SGU
NA_HEADS = D_NA // HEAD_DIM
D_IN = 2 * D_SGU + 3 * D_NA
CHUNK = 128
NA_KH_MAX = 8
NA_KW = 16
NA_QCOLS = 16
NA_KCOLS = 2 * NA_KW
N_COL_BLOCKS = GRID_W // NA_QCOLS
XA_HEADS = 4
XA_HEAD_DIM = 128
D_XA = XA_HEADS * XA_HEAD_DIM
N_GROUPS = 4
EXPERTS_PER_GROUP = 8
N_EXPERTS = N_GROUPS * EXPERTS_PER_GROUP
TOP_K_IN_GROUP = 2
D_EXPERT = 128
EPS = 1e-6

kernel_name = 'hymba_gmlp_natten_hmoe_encoder'


def rmsnorm(x, g):
    xf = x.astype(jnp.float32)
    y = xf * lax.rsqrt(jnp.mean(xf * xf, axis=-1, keepdims=True) + EPS)
    return (y * g.astype(jnp.float32)).astype(x.dtype)


def _na_col_tables():
    c0 = np.arange(N_COL_BLOCKS) * NA_QCOLS
    cs = np.clip(c0 - NA_KW // 2, 0, GRID_W - NA_KCOLS)
    qcol = c0[:, None] + np.arange(NA_QCOLS)[None, :]
    kcol = cs[:, None] + np.arange(NA_KCOLS)[None, :]
    wstart = np.clip(qcol - NA_KW // 2, 0, GRID_W - NA_KW)
    kc = kcol[:, None, :]
    inwin = (kc >= wstart[..., None]) & (kc < wstart[..., None] + NA_KW)
    rel = kc - qcol[:, :, None]
    rel_idx = np.clip(rel, -(NA_KW - 1), NA_KW - 1) + NA_KW - 1
    return kcol, inwin, rel_idx


def spatial_gating(a, w_s, b_s, g_v):
    bsz, t = a.shape[:2]
    u, v = jnp.split(a, 2, axis=-1)
    v = rmsnorm(v, g_v).reshape(bsz, t // CHUNK, CHUNK, SGU_HEADS, HEAD_DIM)
    mixed = jnp.einsum('hpq,bcqhd->bcphd', w_s, v) + b_s.T[None, None, :, :, None]
    return u * mixed.reshape(bsz, t, D_SGU)


def neighbourhood_attention(q, k, v, rpb):
    bsz, t = q.shape[:2]
    rows = t // GRID_W
    kh = min(NA_KH_MAX, rows)
    qg = q.reshape(bsz, rows, GRID_W, NA_HEADS, HEAD_DIM)
    kg = k.reshape(bsz, rows, GRID_W, NA_HEADS, HEAD_DIM)
    vg = v.reshape(bsz, rows, GRID_W, NA_HEADS, HEAD_DIM)
    kcol, inwin, rel_col = _na_col_tables()
    col_bias = rpb[:, :, rel_col]
    mask = jnp.asarray(inwin)[None, None, :, :, None, :]
    scale = HEAD_DIM ** -0.5

    def row_block(r):
        rs = jnp.clip(r - kh // 2, 0, rows - kh)
        qr = lax.dynamic_index_in_dim(qg, r, axis=1, keepdims=False)
        qr = qr.reshape(bsz, N_COL_BLOCKS, NA_QCOLS, NA_HEADS, HEAD_DIM)
        kr = lax.dynamic_slice_in_dim(kg, rs, kh, axis=1)[:, :, kcol]
        vr = lax.dynamic_slice_in_dim(vg, rs, kh, axis=1)[:, :, kcol]
        s = jnp.einsum('bnqhd,brnchd->bhnqrc', qr, kr).astype(jnp.float32) * scale
        row_off = rs + jnp.arange(kh) - r + NA_KH_MAX - 1
        bias = jnp.take(col_bias, row_off, axis=1).transpose(0, 2, 3, 1, 4)
        s = jnp.where(mask, s + bias.astype(jnp.float32), -jnp.inf)
        p = jax.nn.softmax(s.reshape(s.shape[:4] + (-1,)), axis=-1).reshape(s.shape).astype(vr.dtype)
        o = jnp.einsum('bhnqrc,brnchd->bnqhd', p, vr)
        return o.reshape(bsz, GRID_W, D_NA)

    out = lax.map(row_block, jnp.arange(rows))
    return out.transpose(1, 0, 2, 3).reshape(bsz, t, D_NA)


def parallel_mixer(hn, w_in, g_sgu_v, sgu_w, sgu_b, na_rpb, g_out_a, g_out_b, w_out):
    bsz, t, _ = hn.shape
    proj = hn @ w_in
    a = jax.nn.gelu(proj[..., :2 * D_SGU])
    y_a = spatial_gating(a, sgu_w, sgu_b, g_sgu_v)
    q, k, v = jnp.split(proj[..., 2 * D_SGU:], 3, axis=-1)
    shp = (bsz, t, NA_HEADS, HEAD_DIM)
    y_b = neighbourhood_attention(q.reshape(shp), k.reshape(shp), v.reshape(shp), na_rpb)
    y = jnp.concatenate([rmsnorm(y_a, g_out_a), rmsnorm(y_b, g_out_b)], axis=-1)
    return y @ w_out


def memory_cross_attention(hn, memn, w_q, w_kv, w_o):
    bsz, t, _ = hn.shape
    q = (hn @ w_q).reshape(bsz, t, XA_HEADS, XA_HEAD_DIM)
    k, v = jnp.split(memn @ w_kv, 2, axis=-1)
    k = k.reshape(bsz, N_MEM, XA_HEADS, XA_HEAD_DIM)
    v = v.reshape(bsz, N_MEM, XA_HEADS, XA_HEAD_DIM)
    s = jnp.einsum('bthd,bmhd->bhtm', q, k).astype(jnp.float32) * (XA_HEAD_DIM ** -0.5)
    p = jax.nn.softmax(s, axis=-1).astype(v.dtype)
    o = jnp.einsum('bhtm,bmhd->bthd', p, v).reshape(bsz, t, D_XA)
    return o @ w_o


def hierarchical_moe(hn, w_rg, w_re, w_gate, w_up, w_down):
    bsz, t, d = hn.shape
    xf = hn.reshape(-1, d)
    g_prob = jax.nn.softmax((xf @ w_rg).astype(jnp.float32), axis=-1)
    g_top, g_idx = lax.top_k(g_prob, 1)
    e_logits = (xf @ w_re).astype(jnp.float32).reshape(-1, N_GROUPS, EXPERTS_PER_GROUP)
    e_sel = jnp.take_along_axis(e_logits, g_idx[:, :, None], axis=1)[:, 0]
    e_top, e_idx = lax.top_k(jax.nn.softmax(e_sel, axis=-1), TOP_K_IN_GROUP)
    e_top = e_top / jnp.sum(e_top, axis=-1, keepdims=True)
    weights = g_top * e_top
    expert_id = g_idx * EXPERTS_PER_GROUP + e_idx
    gates = jnp.sum(jax.nn.one_hot(expert_id, N_EXPERTS, dtype=jnp.float32) * weights[..., None], axis=1)
    hg = jnp.einsum('nd,edf->nef', xf, w_gate)
    hu = jnp.einsum('nd,edf->nef', xf, w_up)
    act = jax.nn.silu(hg) * hu * gates[..., None].astype(hn.dtype)
    out = jnp.einsum('nef,efd->nd', act, w_down)
    return out.reshape(bsz, t, d)


def encoder_trunk(x, mem, g_mix, w_in, g_sgu_v, sgu_w, sgu_b, na_rpb, g_out_a, g_out_b, w_out,
                  g_xattn, g_mem, w_xq, w_xkv, w_xo, g_moe, w_router_group, w_router_expert,
                  w_exp_gate, w_exp_up, w_exp_down, g_final):
    h = x
    for l in range(DEPTH):
        h = h + parallel_mixer(rmsnorm(h, g_mix[l]), w_in[l], g_sgu_v[l], sgu_w[l], sgu_b[l],
                               na_rpb[l], g_out_a[l], g_out_b[l], w_out[l])
        h = h + memory_cross_attention(rmsnorm(h, g_xattn[l]), rmsnorm(mem, g_mem[l]),
                                       w_xq[l], w_xkv[l], w_xo[l])
        h = h + hierarchical_moe(rmsnorm(h, g_moe[l]), w_router_group[l], w_router_expert[l],
                                 w_exp_gate[l], w_exp_up[l], w_exp_down[l])
    return rmsnorm(h, g_final)


def setup_inputs(seed: int = 0) -> dict:
    key = jax.random.key(seed)
    ks = jax.random.split(key, 32)
    nrm = lambda k, shape, s: jax.random.normal(k, shape, jnp.float32) * s
    gain = lambda k, shape: 1.0 + 0.02 * jax.random.normal(k, shape, jnp.float32)
    L = DEPTH
    return {
        'x_prompt': nrm(ks[0], (BATCH, SEQ, D_MODEL), 1.0),
        'x_sample': nrm(ks[1], (DEC_BATCH, DEC_SEQ, D_MODEL), 1.0),
        'mem_prompt': nrm(ks[2], (BATCH, N_MEM, D_MODEL), 1.0),
        'mem_sample': nrm(ks[3], (DEC_BATCH, N_MEM, D_MODEL), 1.0),
        'g_mix': gain(ks[4], (L, D_MODEL)),
        'w_in': nrm(ks[5], (L, D_MODEL, D_IN), D_MODEL ** -0.5),
        'g_sgu_v': gain(ks[6], (L, D_SGU)),
        'sgu_w': nrm(ks[7], (L, SGU_HEADS, CHUNK, CHUNK), CHUNK ** -0.5),
        'sgu_b': 1.0 + nrm(ks[8], (L, SGU_HEADS, CHUNK), 0.02),
        'na_rpb': nrm(ks[9], (L, NA_HEADS, 2 * NA_KH_MAX - 1, 2 * NA_KW - 1), 0.1),
        'g_out_a': gain(ks[10], (L, D_SGU)),
        'g_out_b': gain(ks[11], (L, D_NA)),
        'w_out': nrm(ks[12], (L, D_MIX, D_MODEL), D_MIX ** -0.5),
        'g_xattn': gain(ks[13], (L, D_MODEL)),
        'g_mem': gain(ks[14], (L, D_MODEL)),
        'w_xq': nrm(ks[15], (L, D_MODEL, D_XA), D_MODEL ** -0.5),
        'w_xkv': nrm(ks[16], (L, D_MODEL, 2 * D_XA), D_MODEL ** -0.5),
        'w_xo': nrm(ks[17], (L, D_XA, D_MODEL), D_XA ** -0.5),
        'g_moe': gain(ks[18], (L, D_MODEL)),
        'w_router_group': nrm(ks[19], (L, D_MODEL, N_GROUPS), D_MODEL ** -0.5),
        'w_router_expert': nrm(ks[20], (L, D_MODEL, N_EXPERTS), D_MODEL ** -0.5),
        'w_exp_gate': nrm(ks[21], (L, N_EXPERTS, D_MODEL, D_EXPERT), D_MODEL ** -0.5),
        'w_exp_up': nrm(ks[22], (L, N_EXPERTS, D_MODEL, D_EXPERT), D_MODEL ** -0.5),
        'w_exp_down': nrm(ks[23], (L, N_EXPERTS, D_EXPERT, D_MODEL), D_EXPERT ** -0.5),
        'g_final': gain(ks[24], (D_MODEL,)),
    }


def reference(x_prompt, x_sample, mem_prompt, mem_sample, g_mix, w_in, g_sgu_v, sgu_w, sgu_b, na_rpb,
              g_out_a, g_out_b, w_out, g_xattn, g_mem, w_xq, w_xkv, w_xo, g_moe, w_router_group,
              w_router_expert, w_exp_gate, w_exp_up, w_exp_down, g_final):
    y_prompt = encoder_trunk(x_prompt, mem_prompt, g_mix, w_in, g_sgu_v, sgu_w, sgu_b, na_rpb, g_out_a,
                             g_out_b, w_out, g_xattn, g_mem, w_xq, w_xkv, w_xo, g_moe, w_router_group,
                             w_router_expert, w_exp_gate, w_exp_up, w_exp_down, g_final)
    y_sample = encoder_trunk(x_sample, mem_sample, g_mix, w_in, g_sgu_v, sgu_w, sgu_b, na_rpb, g_out_a,
                             g_out_b, w_out, g_xattn, g_mem, w_xq, w_xkv, w_xo, g_moe, w_router_group,
                             w_router_expert, w_exp_gate, w_exp_up, w_exp_down, g_final)
    return (y_prompt, y_sample)
```

```python
import functools

import jax
import jax.numpy as jnp
import numpy as np
from jax import lax
from jax.experimental import pallas as pl
from jax.experimental.pallas import tpu as pltpu

F32 = jnp.float32
BF16 = jnp.bfloat16

D_MODEL = 1024
N_MEM = 256
GRID_W = 64
HEAD_DIM = 64
D_SGU = 512
D_NA = 512
CHUNK = 128
NA_KH = 8
NA_KW = 16
XA_HEADS = 4
XA_HEAD_DIM = 128
D_XA = 512
N_GROUPS = 4
EXPERTS_PER_GROUP = 8
N_EXPERTS = 32
D_EXPERT = 128
EPS = 1e-6

LANES = 128
NEG_BIG = -1e30
TOKEN_TILE = 512
NA_ROW_TILE = 8
VMEM_LIMIT_BYTES = 56 * 1024 * 1024


def _rms(x, g):
    return x * lax.rsqrt(jnp.mean(x * x, axis=-1, keepdims=True) + EPS) * g


def _dot(a, b):
    return jnp.dot(a, b, preferred_element_type=F32)


def _dot_nt(a, b):
    return lax.dot_general(a, b, (((1,), (1,)), ((), ())), preferred_element_type=F32)


def _whole(shape):
    nd = len(shape)
    return pl.BlockSpec(shape, lambda *_: (0,) * nd)


def _params(*sem):
    return pltpu.CompilerParams(dimension_semantics=sem, vmem_limit_bytes=VMEM_LIMIT_BYTES)


def _mem_kv_body(mem_ref, g_ref, w_ref, k_ref, v_ref):
    memn = _rms(mem_ref[0], g_ref[...]).astype(BF16)
    kv = _dot(memn, w_ref[...])
    k_ref[0] = kv[:, :D_XA].astype(BF16)
    v_ref[0] = kv[:, D_XA:].astype(BF16)


def _mem_kv(mem, g_mem, w_xkv):
    b = mem.shape[0]
    out = jax.ShapeDtypeStruct((b, N_MEM, D_XA), BF16)
    return pl.pallas_call(
        _mem_kv_body,
        grid=(b,),
        in_specs=[pl.BlockSpec((1, N_MEM, D_MODEL), lambda i: (i, 0, 0)),
                  _whole((1, D_MODEL)), _whole((D_MODEL, 2 * D_XA))],
        out_specs=[pl.BlockSpec((1, N_MEM, D_XA), lambda i: (i, 0, 0))] * 2,
        out_shape=[out, out],
        compiler_params=_params("parallel"),
        name="mem_kv",
    )(mem, g_mem, w_xkv)


def _in_proj_body(x_ref, gmix_ref, win_ref, gv_ref, wcat_ref, bexp_ref, goa_ref,
                  q_ref, k_ref, v_ref, ya_ref):
    tm = x_ref.shape[0]
    hn = _rms(x_ref[...], gmix_ref[...]).astype(BF16)
    a = jax.nn.gelu(_dot(hn, win_ref[:, 0:2 * D_SGU]))
    u = a[:, :D_SGU]
    vn = _rms(a[:, D_SGU:], gv_ref[...]).astype(BF16)
    lo = lax.broadcasted_iota(jnp.int32, (CHUNK, LANES), 1) < HEAD_DIM
    zero = jnp.zeros((CHUNK, LANES), BF16)
    chunks = []
    for c in range(tm // CHUNK):
        pairs = []
        for p in range(D_SGU // LANES):
            vp = vn[c * CHUNK:(c + 1) * CHUNK, p * LANES:(p + 1) * LANES]
            rhs = jnp.concatenate([jnp.where(lo, vp, zero), jnp.where(lo, zero, vp)], axis=0)
            pairs.append(_dot(wcat_ref[p], rhs))
        chunks.append(jnp.concatenate(pairs, axis=1) + bexp_ref[...])
    ya = u * jnp.concatenate(chunks, axis=0)
    ya_ref[...] = _rms(ya, goa_ref[...]).astype(BF16)
    qkv = _dot(hn, win_ref[:, 2 * D_SGU:])
    q_ref[...] = (qkv[:, 0:D_NA] * (HEAD_DIM ** -0.5)).astype(BF16)
    k_ref[...] = qkv[:, D_NA:2 * D_NA].astype(BF16)
    v_ref[...] = qkv[:, 2 * D_NA:].astype(BF16)


def _in_proj(x2, g_mix, w_in, g_sgu_v, wcat, bexp, g_out_a):
    n = x2.shape[0]
    tm = TOKEN_TILE
    out = jax.ShapeDtypeStruct((n, D_NA), BF16)
    tok = lambda w: pl.BlockSpec((tm, w), lambda i: (i, 0))
    return pl.pallas_call(
        _in_proj_body,
        grid=(n // tm,),
        in_specs=[tok(D_MODEL), _whole((1, D_MODEL)), _whole(w_in.shape), _whole((1, D_SGU)),
                  _whole(wcat.shape), _whole(bexp.shape), _whole((1, D_SGU))],
        out_specs=[tok(D_NA)] * 4,
        out_shape=[out] * 4,
        compiler_params=_params("parallel"),
        name="in_proj",
    )(x2, g_mix, w_in, g_sgu_v, wcat, bexp, g_out_a)


def _na_body(rows, q_ref, k_ref, v_ref, bm_ref, o_ref):
    r0 = pl.program_id(2) * NA_ROW_TILE
    lo = lax.broadcasted_iota(jnp.int32, (GRID_W, LANES), 1) < HEAD_DIM
    zero = jnp.zeros((GRID_W, LANES), BF16)
    for i in range(NA_ROW_TILE):
        r = r0 + i
        rs = jnp.clip(r - NA_KH // 2, 0, rows - NA_KH)
        qrow = q_ref[0, i * GRID_W:(i + 1) * GRID_W, :]
        q2 = jnp.concatenate([jnp.where(lo, qrow, zero), jnp.where(lo, zero, qrow)], axis=0)
        start = pl.multiple_of(rs * GRID_W, GRID_W)
        kw = k_ref[0, pl.ds(start, NA_KH * GRID_W), :]
        vw = v_ref[0, pl.ds(start, NA_KH * GRID_W), :]
        s = _dot_nt(q2, kw) + bm_ref[0, r - rs]
        e = jnp.exp(s - jnp.max(s, axis=-1, keepdims=True))
        o = _dot(e.astype(BF16), vw) / jnp.sum(e, axis=-1, keepdims=True)
        o_ref[0, i * GRID_W:(i + 1) * GRID_W, :] = jnp.where(lo, o[:GRID_W], o[GRID_W:]).astype(BF16)


def _na(q, k, v, bm):
    b, t, _ = q.shape
    rows = t // GRID_W
    assert rows >= 2 * NA_KH and rows % NA_ROW_TILE == 0
    tm = NA_ROW_TILE * GRID_W
    n_pairs = D_NA // LANES
    return pl.pallas_call(
        functools.partial(_na_body, rows),
        grid=(b, n_pairs, rows // NA_ROW_TILE),
        in_specs=[pl.BlockSpec((1, tm, LANES), lambda bi, p, ti: (bi, ti, p)),
                  pl.BlockSpec((1, t, LANES), lambda bi, p, ti: (bi, 0, p)),
                  pl.BlockSpec((1, t, LANES), lambda bi, p, ti: (bi, 0, p)),
                  pl.BlockSpec((1, NA_KH, 2 * GRID_W, NA_KH * GRID_W), lambda bi, p, ti: (p, 0, 0, 0))],
        out_specs=pl.BlockSpec((1, tm, LANES), lambda bi, p, ti: (bi, ti, p)),
        out_shape=jax.ShapeDtypeStruct((b, t, D_NA), BF16),
        compiler_params=_params("parallel", "parallel", "arbitrary"),
        name="na",
    )(q, k, v, bm)


def _na_bias_table(rpb):
    c = np.arange(GRID_W)
    wstart = np.clip(c - NA_KW // 2, 0, GRID_W - NA_KW)
    kc = c[None, :]
    inwin = (kc >= wstart[:, None]) & (kc < wstart[:, None] + NA_KW)
    rel = np.clip(kc - c[:, None], -(NA_KW - 1), NA_KW - 1) + NA_KW - 1
    tab = jnp.where(jnp.asarray(inwin)[None, None], rpb[:, :, rel], NEG_BIG)
    d = np.arange(NA_KH)[:, None]
    j = np.arange(NA_KH)[None, :]
    row_off = j - d + NA_KH - 1
    bmh = tab[:, row_off]
    h = rpb.shape[0]
    bmh = bmh.transpose(0, 1, 3, 2, 4).reshape(h // 2, 2, NA_KH, GRID_W, NA_KH * GRID_W)
    return bmh.transpose(0, 2, 1, 3, 4).reshape(h // 2, NA_KH, 2 * GRID_W, NA_KH * GRID_W).astype(F32)


def _mix_xattn_body(x_ref, ya_ref, yb_ref, km_ref, vm_ref, gob_ref, wout_ref, gx_ref, wxq_ref,
                    wxo_ref, h2_ref):
    ybn = _rms(yb_ref[0].astype(F32), gob_ref[...]).astype(BF16)
    h1 = x_ref[0] + _dot(ya_ref[0], wout_ref[0:D_SGU, :]) + _dot(ybn, wout_ref[D_SGU:, :])
    hn = _rms(h1, gx_ref[...]).astype(BF16)
    q = _dot(hn, wxq_ref[...]).astype(BF16)
    heads = []
    for h in range(XA_HEADS):
        sl = slice(h * XA_HEAD_DIM, (h + 1) * XA_HEAD_DIM)
        s = _dot_nt(q[:, sl], km_ref[0, :, sl]) * (XA_HEAD_DIM ** -0.5)
        e = jnp.exp(s - jnp.max(s, axis=-1, keepdims=True))
        o = _dot(e.astype(BF16), vm_ref[0, :, sl]) / jnp.sum(e, axis=-1, keepdims=True)
        heads.append(o.astype(BF16))
    h2_ref[0] = h1 + _dot(jnp.concatenate(heads, axis=1), wxo_ref[...])


def _mix_xattn(x, ya, yb, km, vm, g_out_b, w_out, g_xattn, w_xq, w_xo):
    b, t, _ = x.shape
    tm = TOKEN_TILE
    tok = lambda w: pl.BlockSpec((1, tm, w), lambda bi, ti: (bi, ti, 0))
    mem = pl.BlockSpec((1, N_MEM, D_XA), lambda bi, ti: (bi, 0, 0))
    return pl.pallas_call(
        _mix_xattn_body,
        grid=(b, t // tm),
        in_specs=[tok(D_MODEL), tok(D_SGU), tok(D_NA), mem, mem, _whole((1, D_NA)),
                  _whole(w_out.shape), _whole((1, D_MODEL)), _whole(w_xq.shape), _whole(w_xo.shape)],
        out_specs=tok(D_MODEL),
        out_shape=jax.ShapeDtypeStruct((b, t, D_MODEL), F32),
        compiler_params=_params("parallel", "parallel"),
        name="mix_xattn",
    )(x, ya, yb, km, vm, g_out_b, w_out, g_xattn, w_xq, w_xo)


def _route(logits):
    lane = lax.broadcasted_iota(jnp.int32, logits.shape, 1)
    is_g = lane < N_GROUPS
    is_e = (lane >= N_GROUPS) & (lane < N_GROUPS + N_EXPERTS)
    lg = jnp.where(is_g, logits, NEG_BIG)
    gmax = jnp.max(lg, axis=-1, keepdims=True)
    g_top = 1.0 / jnp.sum(jnp.exp(lg - gmax), axis=-1, keepdims=True)
    g_idx = jnp.min(jnp.where(lg == gmax, lane, LANES), axis=-1, keepdims=True)
    in_grp = is_e & (lane >= N_GROUPS + g_idx * EXPERTS_PER_GROUP) & (
        lane < N_GROUPS + (g_idx + 1) * EXPERTS_PER_GROUP)
    le = jnp.where(in_grp, logits, NEG_BIG)
    m1 = jnp.max(le, axis=-1, keepdims=True)
    i1 = jnp.min(jnp.where(le == m1, lane, LANES), axis=-1, keepdims=True)
    le2 = jnp.where(lane == i1, NEG_BIG, le)
    m2 = jnp.max(le2, axis=-1, keepdims=True)
    i2 = jnp.min(jnp.where(le2 == m2, lane, LANES), axis=-1, keepdims=True)
    p2 = jnp.exp(m2 - m1)
    w1 = g_top / (1.0 + p2)
    w2 = g_top * p2 / (1.0 + p2)
    return jnp.where(lane == i1, w1, 0.0) + jnp.where(lane == i2, w2, 0.0)


def _moe_body(h_ref, gm_ref, wr_hi_ref, wr_lo_ref, wg_ref, wu_ref, wd_ref, gf_ref, y_ref, acc_ref):
    tm = h_ref.shape[0]
    h2 = h_ref[...]
    hn = _rms(h2, gm_ref[...])
    hn_hi = hn.astype(BF16)
    hn_lo = (hn - hn_hi.astype(F32)).astype(BF16)
    logits = _dot(hn_hi, wr_hi_ref[...]) + (_dot(hn_hi, wr_lo_ref[...]) + _dot(hn_lo, wr_hi_ref[...]))
    gates = _route(logits)
    acc_ref[...] = jnp.zeros_like(acc_ref)

    def pair(j, carry):
        e0 = N_GROUPS + 2 * j
        lane = lax.broadcasted_iota(jnp.int32, gates.shape, 1)
        g0 = jnp.sum(jnp.where(lane == e0, gates, 0.0), axis=-1, keepdims=True)
        g1 = jnp.sum(jnp.where(lane == e0 + 1, gates, 0.0), axis=-1, keepdims=True)
        first = lax.broadcasted_iota(jnp.int32, (tm, 2 * D_EXPERT), 1) < D_EXPERT
        gexp = jnp.where(first, g0, g1)
        hg = _dot(hn_hi, wg_ref[j])
        hu = _dot(hn_hi, wu_ref[j])
        act = (jax.nn.silu(hg) * hu * gexp).astype(BF16)
        acc_ref[...] += _dot(act, wd_ref[j])
        return carry

    lax.fori_loop(0, N_EXPERTS // 2, pair, 0)
    y_ref[...] = _rms(h2 + acc_ref[...], gf_ref[...])


def _moe(h2, g_moe, wr_hi, wr_lo, wg2, wu2, wd2, g_final):
    n = h2.shape[0]
    tm = TOKEN_TILE
    tok = pl.BlockSpec((tm, D_MODEL), lambda i: (i, 0))
    resident = pl.BlockSpec(memory_space=pltpu.VMEM)
    return pl.pallas_call(
        _moe_body,
        grid=(n // tm,),
        in_specs=[tok, _whole((1, D_MODEL)), resident, resident, resident, resident, resident,
                  _whole((1, D_MODEL))],
        out_specs=tok,
        out_shape=jax.ShapeDtypeStruct((n, D_MODEL), F32),
        scratch_shapes=[pltpu.VMEM((tm, D_MODEL), F32)],
        compiler_params=_params("arbitrary"),
        name="moe",
    )(h2, g_moe, wr_hi, wr_lo, wg2, wu2, wd2, g_final)


def _prep_weights(g_mix, w_in, g_sgu_v, sgu_w, sgu_b, na_rpb, g_out_a, g_out_b, w_out, g_xattn, g_mem,
                  w_xq, w_xkv, w_xo, g_moe, w_router_group, w_router_expert, w_exp_gate, w_exp_up,
                  w_exp_down, g_final):
    row = lambda g: g.reshape(1, -1).astype(F32)
    heads = sgu_w.shape[0]
    wcat = jnp.concatenate([sgu_w[0::2], sgu_w[1::2]], axis=2).astype(BF16)
    bexp = jnp.repeat(sgu_b.T, HEAD_DIM, axis=1).astype(F32)
    assert heads * HEAD_DIM == D_SGU
    wr = jnp.zeros((D_MODEL, LANES), F32)
    wr = wr.at[:, :N_GROUPS].set(w_router_group).at[:, N_GROUPS:N_GROUPS + N_EXPERTS].set(w_router_expert)
    wr_hi = wr.astype(BF16)
    wr_lo = (wr - wr_hi.astype(F32)).astype(BF16)
    pair_cols = lambda w: (w.reshape(N_EXPERTS // 2, 2, D_MODEL, D_EXPERT).transpose(0, 2, 1, 3)
                           .reshape(N_EXPERTS // 2, D_MODEL, 2 * D_EXPERT).astype(BF16))
    return dict(
        g_mix=row(g_mix), w_in=w_in.astype(BF16), g_sgu_v=row(g_sgu_v), wcat=wcat, bexp=bexp,
        bm=_na_bias_table(na_rpb), g_out_a=row(g_out_a), g_out_b=row(g_out_b), w_out=w_out.astype(BF16),
        g_xattn=row(g_xattn), g_mem=row(g_mem), w_xq=w_xq.astype(BF16), w_xkv=w_xkv.astype(BF16),
        w_xo=w_xo.astype(BF16), g_moe=row(g_moe), wr_hi=wr_hi, wr_lo=wr_lo,
        wg2=pair_cols(w_exp_gate), wu2=pair_cols(w_exp_up),
        wd2=w_exp_down.reshape(N_EXPERTS // 2, 2 * D_EXPERT, D_MODEL).astype(BF16), g_final=row(g_final))


def _trunk(x, mem, w):
    b, t, _ = x.shape
    km, vm = _mem_kv(mem, w["g_mem"], w["w_xkv"])
    q, k, v, ya = _in_proj(x.reshape(b * t, D_MODEL), w["g_mix"], w["w_in"], w["g_sgu_v"], w["wcat"],
                           w["bexp"], w["g_out_a"])
    seq = lambda a: a.reshape(b, t, -1)
    yb = _na(seq(q), seq(k), seq(v), w["bm"])
    h2 = _mix_xattn(x, seq(ya), yb, km, vm, w["g_out_b"], w["w_out"], w["g_xattn"], w["w_xq"], w["w_xo"])
    y = _moe(h2.reshape(b * t, D_MODEL), w["g_moe"], w["wr_hi"], w["wr_lo"], w["wg2"], w["wu2"], w["wd2"],
             w["g_final"])
    return y.reshape(b, t, D_MODEL)


def kernel(x_prompt, x_sample, mem_prompt, mem_sample, g_mix, w_in, g_sgu_v, sgu_w, sgu_b, na_rpb, g_out_a,
           g_out_b, w_out, g_xattn, g_mem, w_xq, w_xkv, w_xo, g_moe, w_router_group, w_router_expert,
           w_exp_gate, w_exp_up, w_exp_down, g_final):
    assert g_mix.shape[0] == 1
    w = _prep_weights(g_mix[0], w_in[0], g_sgu_v[0], sgu_w[0], sgu_b[0], na_rpb[0], g_out_a[0], g_out_b[0],
                      w_out[0], g_xattn[0], g_mem[0], w_xq[0], w_xkv[0], w_xo[0], g_moe[0],
                      w_router_group[0], w_router_expert[0], w_exp_gate[0], w_exp_up[0], w_exp_down[0],
                      g_final)
    return (_trunk(x_prompt, mem_prompt, w), _trunk(x_sample, mem_sample, w))
```

```python
import functools

import jax
import jax.numpy as jnp
import numpy as np
from jax import lax
from jax.experimental import pallas as pl
from jax.experimental.pallas import tpu as pltpu

F32 = jnp.float32
BF16 = jnp.bfloat16

D_MODEL = 1024
N_MEM = 256
GRID_W = 64
HEAD_DIM = 64
D_SGU = 512
D_NA = 512
CHUNK = 128
NA_KH = 8
NA_KW = 16
XA_HEADS = 4
XA_HEAD_DIM = 128
D_XA = 512
N_GROUPS = 4
EXPERTS_PER_GROUP = 8
N_EXPERTS = 32
D_EXPERT = 128
EPS = 1e-6

LANES = 128
NEG_BIG = -1e30
TOKEN_TILE = 512
NA_ROW_TILE = 8
MOE_UNROLL = 2
VMEM_LIMIT_BYTES = 56 * 1024 * 1024


def _rms(x, g):
    return x * lax.rsqrt(jnp.mean(x * x, axis=-1, keepdims=True) + EPS) * g


def _dot(a, b):
    return jnp.dot(a, b, preferred_element_type=F32)


def _dot_nt(a, b):
    return lax.dot_general(a, b, (((1,), (1,)), ((), ())), preferred_element_type=F32)


def _whole(shape):
    nd = len(shape)
    return pl.BlockSpec(shape, lambda *_: (0,) * nd)


def _params(*sem, flags=None):
    return pltpu.CompilerParams(dimension_semantics=sem, vmem_limit_bytes=VMEM_LIMIT_BYTES, flags=flags)


def _mem_kv_body(mem_ref, g_ref, w_ref, k_ref, v_ref):
    memn = _rms(mem_ref[0], g_ref[...]).astype(BF16)
    kv = _dot(memn, w_ref[...])
    k_ref[0] = kv[:, :D_XA].astype(BF16)
    v_ref[0] = kv[:, D_XA:].astype(BF16)


def _mem_kv(mem, g_mem, w_xkv):
    b = mem.shape[0]
    out = jax.ShapeDtypeStruct((b, N_MEM, D_XA), BF16)
    return pl.pallas_call(
        _mem_kv_body,
        grid=(b,),
        in_specs=[pl.BlockSpec((1, N_MEM, D_MODEL), lambda i: (i, 0, 0)),
                  _whole((1, D_MODEL)), _whole((D_MODEL, 2 * D_XA))],
        out_specs=[pl.BlockSpec((1, N_MEM, D_XA), lambda i: (i, 0, 0))] * 2,
        out_shape=[out, out],
        compiler_params=_params("parallel"),
        name="mem_kv",
    )(mem, g_mem, w_xkv)


def _in_proj_body(x_ref, gmix_ref, wa_ref, wqv_ref, wkt_ref, gv_ref, wcat_ref, bexp_ref, goa_ref,
                  q_ref, kt_ref, v_ref, ya_ref):
    tm = x_ref.shape[0]
    hn = _rms(x_ref[...], gmix_ref[...]).astype(BF16)
    a = jax.nn.gelu(_dot(hn, wa_ref[...]))
    u = a[:, :D_SGU]
    vn = _rms(a[:, D_SGU:], gv_ref[...]).astype(BF16)
    lo = lax.broadcasted_iota(jnp.int32, (CHUNK, LANES), 1) < HEAD_DIM
    zero = jnp.zeros((CHUNK, LANES), BF16)
    chunks = []
    for c in range(tm // CHUNK):
        pairs = []
        for p in range(D_SGU // LANES):
            vp = vn[c * CHUNK:(c + 1) * CHUNK, p * LANES:(p + 1) * LANES]
            rhs = jnp.concatenate([jnp.where(lo, vp, zero), jnp.where(lo, zero, vp)], axis=0)
            pairs.append(_dot(wcat_ref[p], rhs))
        chunks.append(jnp.concatenate(pairs, axis=1) + bexp_ref[...])
    ya = u * jnp.concatenate(chunks, axis=0)
    ya_ref[...] = _rms(ya, goa_ref[...]).astype(BF16)
    qv = _dot(hn, wqv_ref[...])
    q_ref[...] = (qv[:, 0:D_NA] * (HEAD_DIM ** -0.5)).astype(BF16)
    v_ref[...] = qv[:, D_NA:].astype(BF16)
    kt = _dot_nt(wkt_ref[...], hn).astype(BF16)
    for c in range(tm // LANES):
        kt_ref[c] = kt[:, c * LANES:(c + 1) * LANES]


def _in_proj(x2, g_mix, w_a, w_qv, w_kt, g_sgu_v, wcat, bexp, g_out_a):
    n = x2.shape[0]
    tm = TOKEN_TILE
    out = jax.ShapeDtypeStruct((n, D_NA), BF16)
    tok = lambda w: pl.BlockSpec((tm, w), lambda i: (i, 0))
    return pl.pallas_call(
        _in_proj_body,
        grid=(n // tm,),
        in_specs=[tok(D_MODEL), _whole((1, D_MODEL)), _whole(w_a.shape), _whole(w_qv.shape),
                  _whole(w_kt.shape), _whole((1, D_SGU)), _whole(wcat.shape), _whole(bexp.shape),
                  _whole((1, D_SGU))],
        out_specs=[tok(D_NA), pl.BlockSpec((tm // LANES, D_NA, LANES), lambda i: (i, 0, 0)),
                   tok(D_NA), tok(D_NA)],
        out_shape=[out, jax.ShapeDtypeStruct((n // LANES, D_NA, LANES), BF16), out, out],
        compiler_params=_params("parallel"),
        name="in_proj",
    )(x2, g_mix, w_a, w_qv, w_kt, g_sgu_v, wcat, bexp, g_out_a)


ROWS_PER_SLAB = LANES // GRID_W
SLABS_PER_WINDOW = NA_KH // ROWS_PER_SLAB


def _na_body(rows, q_ref, kt_ref, v_ref, bm_ref, o_ref, kk_ref, s_ref, p_ref):
    n_slabs = rows // ROWS_PER_SLAB
    n_tiles = rows // NA_ROW_TILE
    half = LANES // 2

    def shift(s, carry):
        a = kt_ref[s]
        kk_ref[0, s] = a
        kk_ref[1, s] = jnp.concatenate([a[:, half:], kt_ref[s + 1][:, :half]], axis=1)
        return carry
    lax.fori_loop(0, n_slabs - 1, shift, 0, unroll=8)
    last = kt_ref[n_slabs - 1]
    kk_ref[0, n_slabs - 1] = last
    kk_ref[1, n_slabs - 1] = last

    s_ref[...] = jnp.zeros_like(s_ref)
    p_ref[...] = jnp.zeros_like(p_ref)

    lo = lax.broadcasted_iota(jnp.int32, (GRID_W, LANES), 1) < HEAD_DIM
    zero = jnp.zeros((GRID_W, LANES), BF16)

    def first_key_row(r):
        return jnp.clip(r - NA_KH // 2, 0, rows - NA_KH)

    def tok(r):
        return pl.multiple_of(r * GRID_W, GRID_W)

    def step(k, carry):
        slot_a = k % 2
        slot_b = (k + 1) % 2
        tile_a = jnp.minimum(k, n_tiles - 1)
        tile_b = jnp.clip(k - 1, 0, n_tiles - 1)
        tile_c = jnp.clip(k - 2, 0, n_tiles - 1)
        for i in range(NA_ROW_TILE):
            r = tile_c * NA_ROW_TILE + i
            vw = v_ref[0, pl.ds(tok(first_key_row(r)), NA_KH * GRID_W), :]
            o = _dot(p_ref[slot_a, i], vw)
            o_ref[0, pl.ds(tok(r), GRID_W), :] = jnp.where(lo, o[:GRID_W], o[GRID_W:]).astype(BF16)
        for i in range(NA_ROW_TILE):
            r = tile_b * NA_ROW_TILE + i
            s = s_ref[slot_b, i] + bm_ref[0, r - first_key_row(r)]
            e = jnp.exp(s - jnp.max(s, axis=-1, keepdims=True))
            p_ref[slot_b, i] = (e / jnp.sum(e, axis=-1, keepdims=True)).astype(BF16)
        for i in range(NA_ROW_TILE):
            r = tile_a * NA_ROW_TILE + i
            rs = first_key_row(r)
            qrow = q_ref[0, pl.ds(tok(r), GRID_W), :]
            q2 = jnp.concatenate([jnp.where(lo, qrow, zero), jnp.where(lo, zero, qrow)], axis=0)
            odd = rs % ROWS_PER_SLAB
            s0 = rs // ROWS_PER_SLAB
            kw = jnp.concatenate([kk_ref[odd, s0 + j] for j in range(SLABS_PER_WINDOW)], axis=1)
            s_ref[slot_a, i] = _dot(q2, kw)
        return carry

    lax.fori_loop(0, n_tiles + 2, step, 0)


def _na(q, kt, v, bm):
    b, t, _ = q.shape
    rows = t // GRID_W
    assert rows >= 2 * NA_KH and rows % NA_ROW_TILE == 0
    n_pairs = D_NA // LANES
    n_slabs = t // LANES
    seq = pl.BlockSpec((1, t, LANES), lambda bi, p: (bi, 0, p))
    stage = (2, NA_ROW_TILE, 2 * GRID_W, NA_KH * GRID_W)
    return pl.pallas_call(
        functools.partial(_na_body, rows),
        grid=(b, n_pairs),
        in_specs=[seq, pl.BlockSpec((n_slabs, LANES, LANES), lambda bi, p: (bi, p, 0)), seq,
                  pl.BlockSpec((1, NA_KH, 2 * GRID_W, NA_KH * GRID_W), lambda bi, p: (p, 0, 0, 0))],
        out_specs=seq,
        out_shape=jax.ShapeDtypeStruct((b, t, D_NA), BF16),
        scratch_shapes=[pltpu.VMEM((2, n_slabs, LANES, LANES), BF16), pltpu.VMEM(stage, F32),
                        pltpu.VMEM(stage, BF16)],
        compiler_params=_params("parallel", "parallel"),
        name="na",
    )(q, kt, v, bm)


def _na_bias_table(rpb):
    c = np.arange(GRID_W)
    wstart = np.clip(c - NA_KW // 2, 0, GRID_W - NA_KW)
    kc = c[None, :]
    inwin = (kc >= wstart[:, None]) & (kc < wstart[:, None] + NA_KW)
    rel = np.clip(kc - c[:, None], -(NA_KW - 1), NA_KW - 1) + NA_KW - 1
    tab = jnp.where(jnp.asarray(inwin)[None, None], rpb[:, :, rel], NEG_BIG)
    d = np.arange(NA_KH)[:, None]
    j = np.arange(NA_KH)[None, :]
    row_off = j - d + NA_KH - 1
    bmh = tab[:, row_off]
    h = rpb.shape[0]
    bmh = bmh.transpose(0, 1, 3, 2, 4).reshape(h // 2, 2, NA_KH, GRID_W, NA_KH * GRID_W)
    return bmh.transpose(0, 2, 1, 3, 4).reshape(h // 2, NA_KH, 2 * GRID_W, NA_KH * GRID_W).astype(F32)


def _mix_xattn_body(x_ref, ya_ref, yb_ref, km_ref, vm_ref, gob_ref, wout_ref, gx_ref, wxq_ref,
                    wxo_ref, h2_ref):
    ybn = _rms(yb_ref[0].astype(F32), gob_ref[...]).astype(BF16)
    h1 = x_ref[0] + _dot(ya_ref[0], wout_ref[0:D_SGU, :]) + _dot(ybn, wout_ref[D_SGU:, :])
    hn = _rms(h1, gx_ref[...]).astype(BF16)
    q = _dot(hn, wxq_ref[...]).astype(BF16)
    heads = []
    for h in range(XA_HEADS):
        sl = slice(h * XA_HEAD_DIM, (h + 1) * XA_HEAD_DIM)
        s = _dot_nt(q[:, sl], km_ref[0, :, sl]) * (XA_HEAD_DIM ** -0.5)
        e = jnp.exp(s - jnp.max(s, axis=-1, keepdims=True))
        o = _dot(e.astype(BF16), vm_ref[0, :, sl]) / jnp.sum(e, axis=-1, keepdims=True)
        heads.append(o.astype(BF16))
    h2_ref[0] = h1 + _dot(jnp.concatenate(heads, axis=1), wxo_ref[...])


def _mix_xattn(x, ya, yb, km, vm, g_out_b, w_out, g_xattn, w_xq, w_xo):
    b, t, _ = x.shape
    tm = TOKEN_TILE
    tok = lambda w: pl.BlockSpec((1, tm, w), lambda bi, ti: (bi, ti, 0))
    mem = pl.BlockSpec((1, N_MEM, D_XA), lambda bi, ti: (bi, 0, 0))
    return pl.pallas_call(
        _mix_xattn_body,
        grid=(b, t // tm),
        in_specs=[tok(D_MODEL), tok(D_SGU), tok(D_NA), mem, mem, _whole((1, D_NA)),
                  _whole(w_out.shape), _whole((1, D_MODEL)), _whole(w_xq.shape), _whole(w_xo.shape)],
        out_specs=tok(D_MODEL),
        out_shape=jax.ShapeDtypeStruct((b, t, D_MODEL), F32),
        compiler_params=_params("parallel", "parallel"),
        name="mix_xattn",
    )(x, ya, yb, km, vm, g_out_b, w_out, g_xattn, w_xq, w_xo)


def _route(logits):
    lane = lax.broadcasted_iota(jnp.int32, logits.shape, 1)
    is_g = lane < N_GROUPS
    is_e = (lane >= N_GROUPS) & (lane < N_GROUPS + N_EXPERTS)
    lg = jnp.where(is_g, logits, NEG_BIG)
    gmax = jnp.max(lg, axis=-1, keepdims=True)
    g_top = 1.0 / jnp.sum(jnp.exp(lg - gmax), axis=-1, keepdims=True)
    g_idx = jnp.min(jnp.where(lg == gmax, lane, LANES), axis=-1, keepdims=True)
    in_grp = is_e & (lane >= N_GROUPS + g_idx * EXPERTS_PER_GROUP) & (
        lane < N_GROUPS + (g_idx + 1) * EXPERTS_PER_GROUP)
    le = jnp.where(in_grp, logits, NEG_BIG)
    m1 = jnp.max(le, axis=-1, keepdims=True)
    i1 = jnp.min(jnp.where(le == m1, lane, LANES), axis=-1, keepdims=True)
    le2 = jnp.where(lane == i1, NEG_BIG, le)
    m2 = jnp.max(le2, axis=-1, keepdims=True)
    i2 = jnp.min(jnp.where(le2 == m2, lane, LANES), axis=-1, keepdims=True)
    p2 = jnp.exp(m2 - m1)
    w1 = g_top / (1.0 + p2)
    w2 = g_top * p2 / (1.0 + p2)
    return jnp.where(lane == i1, w1, 0.0) + jnp.where(lane == i2, w2, 0.0)


def _moe_body(h_ref, gm_ref, wr_ref, wg_ref, wu_ref, wd_ref, gf_ref, y_ref, acc_ref):
    tm = h_ref.shape[0]
    h2 = h_ref[...]
    hn = _rms(h2, gm_ref[...])
    hn_hi = hn.astype(BF16)
    hn_lo = (hn - hn_hi.astype(F32)).astype(BF16)
    part = _dot(hn_hi, wr_ref[...]) + _dot(hn_lo, wr_ref[...])
    logits = part[:, :LANES] + part[:, LANES:]
    gates = _route(logits)
    acc_ref[...] = jnp.zeros_like(acc_ref)

    def pair(j, carry):
        e0 = N_GROUPS + 2 * j
        lane = lax.broadcasted_iota(jnp.int32, gates.shape, 1)
        g0 = jnp.sum(jnp.where(lane == e0, gates, 0.0), axis=-1, keepdims=True)
        g1 = jnp.sum(jnp.where(lane == e0 + 1, gates, 0.0), axis=-1, keepdims=True)
        first = lax.broadcasted_iota(jnp.int32, (tm, 2 * D_EXPERT), 1) < D_EXPERT
        gexp = jnp.where(first, g0, g1)
        hg = _dot(hn_hi, wg_ref[j])
        hu = _dot(hn_hi, wu_ref[j])
        act = (jax.nn.silu(hg) * hu * gexp).astype(BF16)
        acc_ref[...] += _dot(act, wd_ref[j])
        return carry

    lax.fori_loop(0, N_EXPERTS // 2, pair, 0, unroll=MOE_UNROLL)
    y_ref[...] = _rms(h2 + acc_ref[...], gf_ref[...])


def _moe(h2, g_moe, wr, wg2, wu2, wd2, g_final):
    n = h2.shape[0]
    tm = TOKEN_TILE
    tok = pl.BlockSpec((tm, D_MODEL), lambda i: (i, 0))
    resident = pl.BlockSpec(memory_space=pltpu.VMEM)
    return pl.pallas_call(
        _moe_body,
        grid=(n // tm,),
        in_specs=[tok, _whole((1, D_MODEL)), resident, resident, resident, resident, _whole((1, D_MODEL))],
        out_specs=tok,
        out_shape=jax.ShapeDtypeStruct((n, D_MODEL), F32),
        scratch_shapes=[pltpu.VMEM((tm, D_MODEL), F32)],
        compiler_params=_params("arbitrary"),
        name="moe",
    )(h2, g_moe, wr, wg2, wu2, wd2, g_final)


def _prep_weights(g_mix, w_in, g_sgu_v, sgu_w, sgu_b, na_rpb, g_out_a, g_out_b, w_out, g_xattn, g_mem,
                  w_xq, w_xkv, w_xo, g_moe, w_router_group, w_router_expert, w_exp_gate, w_exp_up,
                  w_exp_down, g_final):
    row = lambda g: g.reshape(1, -1).astype(F32)
    heads = sgu_w.shape[0]
    wcat = jnp.concatenate([sgu_w[0::2], sgu_w[1::2]], axis=2).astype(BF16)
    bexp = jnp.repeat(sgu_b.T, HEAD_DIM, axis=1).astype(F32)
    assert heads * HEAD_DIM == D_SGU
    wr = jnp.zeros((D_MODEL, LANES), F32)
    wr = wr.at[:, :N_GROUPS].set(w_router_group).at[:, N_GROUPS:N_GROUPS + N_EXPERTS].set(w_router_expert)
    wr_hi = wr.astype(BF16)
    wr_split = jnp.concatenate([wr_hi, (wr - wr_hi.astype(F32)).astype(BF16)], axis=1)
    pair_cols = lambda w: (w.reshape(N_EXPERTS // 2, 2, D_MODEL, D_EXPERT).transpose(0, 2, 1, 3)
                           .reshape(N_EXPERTS // 2, D_MODEL, 2 * D_EXPERT).astype(BF16))
    return dict(
        g_mix=row(g_mix), w_a=w_in[:, :2 * D_SGU].astype(BF16),
        w_qv=jnp.concatenate([w_in[:, 2 * D_SGU:2 * D_SGU + D_NA], w_in[:, 2 * D_SGU + 2 * D_NA:]], axis=1).astype(BF16),
        w_kt=w_in[:, 2 * D_SGU + D_NA:2 * D_SGU + 2 * D_NA].T.astype(BF16),
        g_sgu_v=row(g_sgu_v), wcat=wcat, bexp=bexp,
        bm=_na_bias_table(na_rpb), g_out_a=row(g_out_a), g_out_b=row(g_out_b), w_out=w_out.astype(BF16),
        g_xattn=row(g_xattn), g_mem=row(g_mem), w_xq=w_xq.astype(BF16), w_xkv=w_xkv.astype(BF16),
        w_xo=w_xo.astype(BF16), g_moe=row(g_moe), wr=wr_split,
        wg2=pair_cols(w_exp_gate), wu2=pair_cols(w_exp_up),
        wd2=w_exp_down.reshape(N_EXPERTS // 2, 2 * D_EXPERT, D_MODEL).astype(BF16), g_final=row(g_final))


def _trunk(x, mem, w):
    b, t, _ = x.shape
    km, vm = _mem_kv(mem, w["g_mem"], w["w_xkv"])
    q, kt, v, ya = _in_proj(x.reshape(b * t, D_MODEL), w["g_mix"], w["w_a"], w["w_qv"], w["w_kt"],
                            w["g_sgu_v"], w["wcat"], w["bexp"], w["g_out_a"])
    seq = lambda a: a.reshape(b, t, -1)
    yb = _na(seq(q), kt, seq(v), w["bm"])
    h2 = _mix_xattn(x, seq(ya), yb, km, vm, w["g_out_b"], w["w_out"], w["g_xattn"], w["w_xq"], w["w_xo"])
    y = _moe(h2.reshape(b * t, D_MODEL), w["g_moe"], w["wr"], w["wg2"], w["wu2"], w["wd2"], w["g_final"])
    return y.reshape(b, t, D_MODEL)


def kernel(x_prompt, x_sample, mem_prompt, mem_sample, g_mix, w_in, g_sgu_v, sgu_w, sgu_b, na_rpb, g_out_a,
           g_out_b, w_out, g_xattn, g_mem, w_xq, w_xkv, w_xo, g_moe, w_router_group, w_router_expert,
           w_exp_gate, w_exp_up, w_exp_down, g_final):
    assert g_mix.shape[0] == 1
    w = _prep_weights(g_mix[0], w_in[0], g_sgu_v[0], sgu_w[0], sgu_b[0], na_rpb[0], g_out_a[0], g_out_b[0],
                      w_out[0], g_xattn[0], g_mem[0], w_xq[0], w_xkv[0], w_xo[0], g_moe[0],
                      w_router_group[0], w_router_expert[0], w_exp_gate[0], w_exp_up[0], w_exp_down[0],
                      g_final)
    return (_trunk(x_prompt, mem_prompt, w), _trunk(x_sample, mem_sample, w))
```

```python
import functools

import jax
import jax.numpy as jnp
import numpy as np
from jax import lax
from jax.experimental import pallas as pl
from jax.experimental.pallas import tpu as pltpu

F32 = jnp.float32
BF16 = jnp.bfloat16

D_MODEL = 1024
N_MEM = 256
GRID_W = 64
HEAD_DIM = 64
D_SGU = 512
D_NA = 512
CHUNK = 128
NA_KH = 8
NA_KW = 16
XA_HEADS = 4
XA_HEAD_DIM = 128
D_XA = 512
N_GROUPS = 4
EXPERTS_PER_GROUP = 8
N_EXPERTS = 32
D_EXPERT = 128
EPS = 1e-6

LANES = 128
NEG_BIG = -1e30
TOKEN_TILE = 1024
NA_ROW_TILE = 8
MOE_UNROLL = 2
VMEM_LIMIT_BYTES = 56 * 1024 * 1024


def _rms(x, g):
    return x * lax.rsqrt(jnp.mean(x * x, axis=-1, keepdims=True) + EPS) * g


def _dot(a, b):
    return jnp.dot(a, b, preferred_element_type=F32)


def _dot_nt(a, b):
    return lax.dot_general(a, b, (((1,), (1,)), ((), ())), preferred_element_type=F32)


def _whole(shape):
    nd = len(shape)
    return pl.BlockSpec(shape, lambda *_: (0,) * nd)


def _params(*sem, flags=None):
    return pltpu.CompilerParams(dimension_semantics=sem, vmem_limit_bytes=VMEM_LIMIT_BYTES, flags=flags)


def _mem_kv_body(mem_ref, g_ref, w_ref, k_ref, v_ref):
    memn = _rms(mem_ref[0], g_ref[...]).astype(BF16)
    kv = _dot(memn, w_ref[...])
    k_ref[0] = kv[:, :D_XA].astype(BF16)
    v_ref[0] = kv[:, D_XA:].astype(BF16)


def _mem_kv(mem, g_mem, w_xkv):
    b = mem.shape[0]
    out = jax.ShapeDtypeStruct((b, N_MEM, D_XA), BF16)
    return pl.pallas_call(
        _mem_kv_body,
        grid=(b,),
        in_specs=[pl.BlockSpec((1, N_MEM, D_MODEL), lambda i: (i, 0, 0)),
                  _whole((1, D_MODEL)), _whole((D_MODEL, 2 * D_XA))],
        out_specs=[pl.BlockSpec((1, N_MEM, D_XA), lambda i: (i, 0, 0))] * 2,
        out_shape=[out, out],
        compiler_params=_params("parallel"),
        name="mem_kv",
    )(mem, g_mem, w_xkv)


def _in_proj_body(x_ref, gmix_ref, wa_ref, wqv_ref, wkt_ref, gv_ref, wcat_ref, bexp_ref, goa_ref,
                  q_ref, kt_ref, v_ref, ya_ref):
    tm = x_ref.shape[0]
    hn = _rms(x_ref[...], gmix_ref[...]).astype(BF16)
    a = jax.nn.gelu(_dot(hn, wa_ref[...]))
    u = a[:, :D_SGU]
    vn = _rms(a[:, D_SGU:], gv_ref[...]).astype(BF16)
    lo = lax.broadcasted_iota(jnp.int32, (CHUNK, LANES), 1) < HEAD_DIM
    zero = jnp.zeros((CHUNK, LANES), BF16)
    chunks = []
    for c in range(tm // CHUNK):
        pairs = []
        for p in range(D_SGU // LANES):
            vp = vn[c * CHUNK:(c + 1) * CHUNK, p * LANES:(p + 1) * LANES]
            rhs = jnp.concatenate([jnp.where(lo, vp, zero), jnp.where(lo, zero, vp)], axis=0)
            pairs.append(_dot(wcat_ref[p], rhs))
        chunks.append(jnp.concatenate(pairs, axis=1) + bexp_ref[...])
    ya = u * jnp.concatenate(chunks, axis=0)
    ya_ref[...] = _rms(ya, goa_ref[...]).astype(BF16)
    qv = _dot(hn, wqv_ref[...])
    q_ref[...] = (qv[:, 0:D_NA] * (HEAD_DIM ** -0.5)).astype(BF16)
    v_ref[...] = qv[:, D_NA:].astype(BF16)
    kt = _dot_nt(wkt_ref[...], hn).astype(BF16)
    for c in range(tm // LANES):
        kt_ref[c] = kt[:, c * LANES:(c + 1) * LANES]


def _in_proj(x2, g_mix, w_a, w_qv, w_kt, g_sgu_v, wcat, bexp, g_out_a):
    n = x2.shape[0]
    tm = TOKEN_TILE
    out = jax.ShapeDtypeStruct((n, D_NA), BF16)
    tok = lambda w: pl.BlockSpec((tm, w), lambda i: (i, 0))
    return pl.pallas_call(
        _in_proj_body,
        grid=(n // tm,),
        in_specs=[tok(D_MODEL), _whole((1, D_MODEL)), _whole(w_a.shape), _whole(w_qv.shape),
                  _whole(w_kt.shape), _whole((1, D_SGU)), _whole(wcat.shape), _whole(bexp.shape),
                  _whole((1, D_SGU))],
        out_specs=[tok(D_NA), pl.BlockSpec((tm // LANES, D_NA, LANES), lambda i: (i, 0, 0)),
                   tok(D_NA), tok(D_NA)],
        out_shape=[out, jax.ShapeDtypeStruct((n // LANES, D_NA, LANES), BF16), out, out],
        compiler_params=_params("parallel"),
        name="in_proj",
    )(x2, g_mix, w_a, w_qv, w_kt, g_sgu_v, wcat, bexp, g_out_a)


ROWS_PER_SLAB = LANES // GRID_W
SLABS_PER_WINDOW = NA_KH // ROWS_PER_SLAB


def _na_body(rows, q_ref, kt_ref, v_ref, bm_ref, o_ref, kk_ref, s_ref, p_ref):
    n_slabs = rows // ROWS_PER_SLAB
    n_tiles = rows // NA_ROW_TILE
    half = LANES // 2

    def shift(s, carry):
        a = kt_ref[s]
        kk_ref[0, s] = a
        kk_ref[1, s] = jnp.concatenate([a[:, half:], kt_ref[s + 1][:, :half]], axis=1)
        return carry
    lax.fori_loop(0, n_slabs - 1, shift, 0, unroll=8)
    last = kt_ref[n_slabs - 1]
    kk_ref[0, n_slabs - 1] = last
    kk_ref[1, n_slabs - 1] = last

    s_ref[1] = jnp.zeros(s_ref.shape[1:], F32)
    p_ref[0] = jnp.zeros(p_ref.shape[1:], BF16)

    lo = lax.broadcasted_iota(jnp.int32, (GRID_W, LANES), 1) < HEAD_DIM
    zero = jnp.zeros((GRID_W, LANES), BF16)

    def first_key_row(r):
        return jnp.clip(r - NA_KH // 2, 0, rows - NA_KH)

    def tok(r):
        return pl.multiple_of(r * GRID_W, GRID_W)

    def step(k, carry):
        slot_a = k % 2
        slot_b = (k + 1) % 2
        tile_a = jnp.minimum(k, n_tiles - 1)
        tile_b = jnp.clip(k - 1, 0, n_tiles - 1)
        tile_c = jnp.clip(k - 2, 0, n_tiles - 1)
        for i in range(NA_ROW_TILE):
            r = tile_c * NA_ROW_TILE + i
            vw = v_ref[0, pl.ds(tok(first_key_row(r)), NA_KH * GRID_W), :]
            o = _dot(p_ref[slot_a, i], vw)
            o_ref[0, pl.ds(tok(r), GRID_W), :] = jnp.where(lo, o[:GRID_W], o[GRID_W:]).astype(BF16)
        for i in range(NA_ROW_TILE):
            r = tile_b * NA_ROW_TILE + i
            s = s_ref[slot_b, i] + bm_ref[0, r - first_key_row(r)]
            e = jnp.exp(s - jnp.max(s, axis=-1, keepdims=True))
            p_ref[slot_b, i] = (e / jnp.sum(e, axis=-1, keepdims=True)).astype(BF16)
        for i in range(NA_ROW_TILE):
            r = tile_a * NA_ROW_TILE + i
            rs = first_key_row(r)
            qrow = q_ref[0, pl.ds(tok(r), GRID_W), :]
            q2 = jnp.concatenate([jnp.where(lo, qrow, zero), jnp.where(lo, zero, qrow)], axis=0)
            odd = rs % ROWS_PER_SLAB
            s0 = rs // ROWS_PER_SLAB
            kw = jnp.concatenate([kk_ref[odd, s0 + j] for j in range(SLABS_PER_WINDOW)], axis=1)
            s_ref[slot_a, i] = _dot(q2, kw)
        return carry

    lax.fori_loop(0, n_tiles + 2, step, 0)


def _na(q, kt, v, bm):
    b, t, _ = q.shape
    rows = t // GRID_W
    assert rows >= 2 * NA_KH and rows % NA_ROW_TILE == 0
    n_pairs = D_NA // LANES
    n_slabs = t // LANES
    seq = pl.BlockSpec((1, t, LANES), lambda bi, p: (bi, 0, p))
    stage = (2, NA_ROW_TILE, 2 * GRID_W, NA_KH * GRID_W)
    return pl.pallas_call(
        functools.partial(_na_body, rows),
        grid=(b, n_pairs),
        in_specs=[seq, pl.BlockSpec((n_slabs, LANES, LANES), lambda bi, p: (bi, p, 0)), seq,
                  pl.BlockSpec((1, NA_KH, 2 * GRID_W, NA_KH * GRID_W), lambda bi, p: (p, 0, 0, 0))],
        out_specs=seq,
        out_shape=jax.ShapeDtypeStruct((b, t, D_NA), BF16),
        scratch_shapes=[pltpu.VMEM((2, n_slabs, LANES, LANES), BF16), pltpu.VMEM(stage, F32),
                        pltpu.VMEM(stage, BF16)],
        compiler_params=_params("parallel", "parallel"),
        name="na",
    )(q, kt, v, bm)


def _na_bias_table(rpb):
    c = np.arange(GRID_W)
    wstart = np.clip(c - NA_KW // 2, 0, GRID_W - NA_KW)
    kc = c[None, :]
    inwin = (kc >= wstart[:, None]) & (kc < wstart[:, None] + NA_KW)
    rel = np.clip(kc - c[:, None], -(NA_KW - 1), NA_KW - 1) + NA_KW - 1
    tab = jnp.where(jnp.asarray(inwin)[None, None], rpb[:, :, rel], NEG_BIG)
    bmh = jnp.stack([tab[:, NA_KH - 1 - d:2 * NA_KH - 1 - d] for d in range(NA_KH)], axis=1)
    h = rpb.shape[0]
    bmh = bmh.transpose(0, 1, 3, 2, 4).reshape(h // 2, 2, NA_KH, GRID_W, NA_KH * GRID_W)
    return bmh.transpose(0, 2, 1, 3, 4).reshape(h // 2, NA_KH, 2 * GRID_W, NA_KH * GRID_W).astype(F32)


def _mix_xattn_body(x_ref, ya_ref, yb_ref, km_ref, vm_ref, gob_ref, wout_ref, gx_ref, wxq_ref,
                    wxo_ref, h2_ref):
    ybn = _rms(yb_ref[0].astype(F32), gob_ref[...]).astype(BF16)
    h1 = x_ref[0] + _dot(ya_ref[0], wout_ref[0:D_SGU, :]) + _dot(ybn, wout_ref[D_SGU:, :])
    hn = _rms(h1, gx_ref[...]).astype(BF16)
    q = _dot(hn, wxq_ref[...]).astype(BF16)
    heads = []
    for h in range(XA_HEADS):
        sl = slice(h * XA_HEAD_DIM, (h + 1) * XA_HEAD_DIM)
        s = _dot_nt(q[:, sl], km_ref[0, :, sl]) * (XA_HEAD_DIM ** -0.5)
        e = jnp.exp(s - jnp.max(s, axis=-1, keepdims=True))
        o = _dot(e.astype(BF16), vm_ref[0, :, sl]) / jnp.sum(e, axis=-1, keepdims=True)
        heads.append(o.astype(BF16))
    h2_ref[0] = h1 + _dot(jnp.concatenate(heads, axis=1), wxo_ref[...])


def _mix_xattn(x, ya, yb, km, vm, g_out_b, w_out, g_xattn, w_xq, w_xo):
    b, t, _ = x.shape
    tm = TOKEN_TILE
    tok = lambda w: pl.BlockSpec((1, tm, w), lambda bi, ti: (bi, ti, 0))
    mem = pl.BlockSpec((1, N_MEM, D_XA), lambda bi, ti: (bi, 0, 0))
    return pl.pallas_call(
        _mix_xattn_body,
        grid=(b, t // tm),
        in_specs=[tok(D_MODEL), tok(D_SGU), tok(D_NA), mem, mem, _whole((1, D_NA)),
                  _whole(w_out.shape), _whole((1, D_MODEL)), _whole(w_xq.shape), _whole(w_xo.shape)],
        out_specs=tok(D_MODEL),
        out_shape=jax.ShapeDtypeStruct((b, t, D_MODEL), F32),
        compiler_params=_params("parallel", "parallel"),
        name="mix_xattn",
    )(x, ya, yb, km, vm, g_out_b, w_out, g_xattn, w_xq, w_xo)


def _route(logits):
    lane = lax.broadcasted_iota(jnp.int32, logits.shape, 1)
    is_g = lane < N_GROUPS
    is_e = (lane >= N_GROUPS) & (lane < N_GROUPS + N_EXPERTS)
    lg = jnp.where(is_g, logits, NEG_BIG)
    gmax = jnp.max(lg, axis=-1, keepdims=True)
    g_top = 1.0 / jnp.sum(jnp.exp(lg - gmax), axis=-1, keepdims=True)
    g_idx = jnp.min(jnp.where(lg == gmax, lane, LANES), axis=-1, keepdims=True)
    in_grp = is_e & (lane >= N_GROUPS + g_idx * EXPERTS_PER_GROUP) & (
        lane < N_GROUPS + (g_idx + 1) * EXPERTS_PER_GROUP)
    le = jnp.where(in_grp, logits, NEG_BIG)
    m1 = jnp.max(le, axis=-1, keepdims=True)
    i1 = jnp.min(jnp.where(le == m1, lane, LANES), axis=-1, keepdims=True)
    le2 = jnp.where(lane == i1, NEG_BIG, le)
    m2 = jnp.max(le2, axis=-1, keepdims=True)
    i2 = jnp.min(jnp.where(le2 == m2, lane, LANES), axis=-1, keepdims=True)
    p2 = jnp.exp(m2 - m1)
    w1 = g_top / (1.0 + p2)
    w2 = g_top * p2 / (1.0 + p2)
    return jnp.where(lane == i1, w1, 0.0) + jnp.where(lane == i2, w2, 0.0)


def _moe_body(h_ref, gm_ref, wr_ref, wg_ref, wu_ref, wd_ref, gf_ref, y_ref):
    tm = h_ref.shape[0]
    hn = _rms(h_ref[...], gm_ref[...])
    hn_hi = hn.astype(BF16)
    hn_lo = (hn - hn_hi.astype(F32)).astype(BF16)
    part = _dot(hn_hi, wr_ref[...]) + _dot(hn_lo, wr_ref[...])
    logits = part[:, :LANES] + part[:, LANES:]
    gates = _route(logits)
    y_ref[...] = jnp.zeros_like(y_ref)

    def pair(j, carry):
        e0 = N_GROUPS + 2 * j
        lane = lax.broadcasted_iota(jnp.int32, gates.shape, 1)
        g0 = jnp.sum(jnp.where(lane == e0, gates, 0.0), axis=-1, keepdims=True)
        g1 = jnp.sum(jnp.where(lane == e0 + 1, gates, 0.0), axis=-1, keepdims=True)
        first = lax.broadcasted_iota(jnp.int32, (tm, 2 * D_EXPERT), 1) < D_EXPERT
        gexp = jnp.where(first, g0, g1)
        hg = _dot(hn_hi, wg_ref[j])
        hu = _dot(hn_hi, wu_ref[j])
        act = (jax.nn.silu(hg) * hu * gexp).astype(BF16)
        y_ref[...] += _dot(act, wd_ref[j])
        return carry

    lax.fori_loop(0, N_EXPERTS // 2, pair, 0, unroll=MOE_UNROLL)
    y_ref[...] = _rms(h_ref[...] + y_ref[...], gf_ref[...])


def _moe(h2, g_moe, wr, wg2, wu2, wd2, g_final):
    n = h2.shape[0]
    tm = TOKEN_TILE
    tok = pl.BlockSpec((tm, D_MODEL), lambda i: (i, 0))
    resident = pl.BlockSpec(memory_space=pltpu.VMEM)
    return pl.pallas_call(
        _moe_body,
        grid=(n // tm,),
        in_specs=[tok, _whole((1, D_MODEL)), resident, resident, resident, resident, _whole((1, D_MODEL))],
        out_specs=tok,
        out_shape=jax.ShapeDtypeStruct((n, D_MODEL), F32),
        compiler_params=_params("arbitrary"),
        name="moe",
    )(h2, g_moe, wr, wg2, wu2, wd2, g_final)


def _prep_weights(g_mix, w_in, g_sgu_v, sgu_w, sgu_b, na_rpb, g_out_a, g_out_b, w_out, g_xattn, g_mem,
                  w_xq, w_xkv, w_xo, g_moe, w_router_group, w_router_expert, w_exp_gate, w_exp_up,
                  w_exp_down, g_final):
    row = lambda g: g.reshape(1, -1).astype(F32)
    heads = sgu_w.shape[0]
    wcat = jnp.concatenate([sgu_w[0::2], sgu_w[1::2]], axis=2).astype(BF16)
    bexp = jnp.repeat(sgu_b.T, HEAD_DIM, axis=1).astype(F32)
    assert heads * HEAD_DIM == D_SGU
    wr = jnp.zeros((D_MODEL, LANES), F32)
    wr = wr.at[:, :N_GROUPS].set(w_router_group).at[:, N_GROUPS:N_GROUPS + N_EXPERTS].set(w_router_expert)
    wr_hi = wr.astype(BF16)
    wr_split = jnp.concatenate([wr_hi, (wr - wr_hi.astype(F32)).astype(BF16)], axis=1)
    pair_cols = lambda w: (w.reshape(N_EXPERTS // 2, 2, D_MODEL, D_EXPERT).transpose(0, 2, 1, 3)
                           .reshape(N_EXPERTS // 2, D_MODEL, 2 * D_EXPERT).astype(BF16))
    return dict(
        g_mix=row(g_mix), w_a=w_in[:, :2 * D_SGU].astype(BF16),
        w_qv=jnp.concatenate([w_in[:, 2 * D_SGU:2 * D_SGU + D_NA], w_in[:, 2 * D_SGU + 2 * D_NA:]], axis=1).astype(BF16),
        w_kt=w_in[:, 2 * D_SGU + D_NA:2 * D_SGU + 2 * D_NA].T.astype(BF16),
        g_sgu_v=row(g_sgu_v), wcat=wcat, bexp=bexp,
        bm=_na_bias_table(na_rpb), g_out_a=row(g_out_a), g_out_b=row(g_out_b), w_out=w_out.astype(BF16),
        g_xattn=row(g_xattn), g_mem=row(g_mem), w_xq=w_xq.astype(BF16), w_xkv=w_xkv.astype(BF16),
        w_xo=w_xo.astype(BF16), g_moe=row(g_moe), wr=wr_split,
        wg2=pair_cols(w_exp_gate), wu2=pair_cols(w_exp_up),
        wd2=w_exp_down.reshape(N_EXPERTS // 2, 2 * D_EXPERT, D_MODEL).astype(BF16), g_final=row(g_final))


def _trunk(x, mem, w):
    b, t, _ = x.shape
    km, vm = _mem_kv(mem, w["g_mem"], w["w_xkv"])
    q, kt, v, ya = _in_proj(x.reshape(b * t, D_MODEL), w["g_mix"], w["w_a"], w["w_qv"], w["w_kt"],
                            w["g_sgu_v"], w["wcat"], w["bexp"], w["g_out_a"])
    seq = lambda a: a.reshape(b, t, -1)
    yb = _na(seq(q), kt, seq(v), w["bm"])
    h2 = _mix_xattn(x, seq(ya), yb, km, vm, w["g_out_b"], w["w_out"], w["g_xattn"], w["w_xq"], w["w_xo"])
    y = _moe(h2.reshape(b * t, D_MODEL), w["g_moe"], w["wr"], w["wg2"], w["wu2"], w["wd2"], w["g_final"])
    return y.reshape(b, t, D_MODEL)


def kernel(x_prompt, x_sample, mem_prompt, mem_sample, g_mix, w_in, g_sgu_v, sgu_w, sgu_b, na_rpb, g_out_a,
           g_out_b, w_out, g_xattn, g_mem, w_xq, w_xkv, w_xo, g_moe, w_router_group, w_router_expert,
           w_exp_gate, w_exp_up, w_exp_down, g_final):
    assert g_mix.shape[0] == 1
    w = _prep_weights(g_mix[0], w_in[0], g_sgu_v[0], sgu_w[0], sgu_b[0], na_rpb[0], g_out_a[0], g_out_b[0],
                      w_out[0], g_xattn[0], g_mem[0], w_xq[0], w_xkv[0], w_xo[0], g_moe[0],
                      w_router_group[0], w_router_expert[0], w_exp_gate[0], w_exp_up[0], w_exp_down[0],
                      g_final)
    return (_trunk(x_prompt, mem_prompt, w), _trunk(x_sample, mem_sample, w))
```

```python
import functools

import jax
import jax.numpy as jnp
import numpy as np
from jax import lax
from jax.experimental import pallas as pl
from jax.experimental.pallas import tpu as pltpu

F32 = jnp.float32
BF16 = jnp.bfloat16

D_MODEL = 1024
N_MEM = 256
GRID_W = 64
HEAD_DIM = 64
D_SGU = 512
D_NA = 512
CHUNK = 128
NA_KH = 8
NA_KW = 16
XA_HEADS = 4
XA_HEAD_DIM = 128
D_XA = 512
N_GROUPS = 4
EXPERTS_PER_GROUP = 8
N_EXPERTS = 32
D_EXPERT = 128
EPS = 1e-6

LANES = 128
NEG_BIG = -1e30
TOKEN_TILE = 1024
NA_ROW_TILE = 8
VMEM_LIMIT_BYTES = 56 * 1024 * 1024


def _rms(x, g):
    return x * lax.rsqrt(jnp.mean(x * x, axis=-1, keepdims=True) + EPS) * g


def _dot(a, b):
    return jnp.dot(a, b, preferred_element_type=F32)


def _dot_nt(a, b):
    return lax.dot_general(a, b, (((1,), (1,)), ((), ())), preferred_element_type=F32)


def _whole(shape):
    nd = len(shape)
    return pl.BlockSpec(shape, lambda *_: (0,) * nd)


def _params(*sem, flags=None):
    return pltpu.CompilerParams(dimension_semantics=sem, vmem_limit_bytes=VMEM_LIMIT_BYTES, flags=flags)


def _mem_kv_body(mem_ref, g_ref, w_ref, k_ref, v_ref):
    memn = _rms(mem_ref[0], g_ref[...]).astype(BF16)
    kv = _dot(memn, w_ref[...])
    k_ref[0] = kv[:, :D_XA].astype(BF16)
    v_ref[0] = kv[:, D_XA:].astype(BF16)


def _mem_kv(mem, g_mem, w_xkv):
    b = mem.shape[0]
    out = jax.ShapeDtypeStruct((b, N_MEM, D_XA), BF16)
    return pl.pallas_call(
        _mem_kv_body,
        grid=(b,),
        in_specs=[pl.BlockSpec((1, N_MEM, D_MODEL), lambda i: (i, 0, 0)),
                  _whole((1, D_MODEL)), _whole((D_MODEL, 2 * D_XA))],
        out_specs=[pl.BlockSpec((1, N_MEM, D_XA), lambda i: (i, 0, 0))] * 2,
        out_shape=[out, out],
        compiler_params=_params("parallel"),
        name="mem_kv",
    )(mem, g_mem, w_xkv)


def _in_proj_body(x_ref, gmix_ref, wa_ref, wqv_ref, wkt_ref, gv_ref, wcat_ref, bexp_ref, goa_ref,
                  q_ref, kt_ref, v_ref, ya_ref):
    tm = x_ref.shape[0]
    hn = _rms(x_ref[...], gmix_ref[...]).astype(BF16)
    a = jax.nn.gelu(_dot(hn, wa_ref[...]))
    u = a[:, :D_SGU]
    vn = _rms(a[:, D_SGU:], gv_ref[...]).astype(BF16)
    lo = lax.broadcasted_iota(jnp.int32, (CHUNK, LANES), 1) < HEAD_DIM
    zero = jnp.zeros((CHUNK, LANES), BF16)
    chunks = []
    for c in range(tm // CHUNK):
        pairs = []
        for p in range(D_SGU // LANES):
            vp = vn[c * CHUNK:(c + 1) * CHUNK, p * LANES:(p + 1) * LANES]
            rhs = jnp.concatenate([jnp.where(lo, vp, zero), jnp.where(lo, zero, vp)], axis=0)
            pairs.append(_dot(wcat_ref[p], rhs))
        chunks.append(jnp.concatenate(pairs, axis=1) + bexp_ref[...])
    ya = u * jnp.concatenate(chunks, axis=0)
    ya_ref[...] = _rms(ya, goa_ref[...]).astype(BF16)
    qv = _dot(hn, wqv_ref[...])
    q_ref[...] = (qv[:, 0:D_NA] * (HEAD_DIM ** -0.5)).astype(BF16)
    v_ref[...] = qv[:, D_NA:].astype(BF16)
    kt = _dot_nt(wkt_ref[...], hn).astype(BF16)
    for c in range(tm // LANES):
        kt_ref[c] = kt[:, c * LANES:(c + 1) * LANES]


def _in_proj(x2, g_mix, w_a, w_qv, w_kt, g_sgu_v, wcat, bexp, g_out_a):
    n = x2.shape[0]
    tm = TOKEN_TILE
    out = jax.ShapeDtypeStruct((n, D_NA), BF16)
    tok = lambda w: pl.BlockSpec((tm, w), lambda i: (i, 0))
    return pl.pallas_call(
        _in_proj_body,
        grid=(n // tm,),
        in_specs=[tok(D_MODEL), _whole((1, D_MODEL)), _whole(w_a.shape), _whole(w_qv.shape),
                  _whole(w_kt.shape), _whole((1, D_SGU)), _whole(wcat.shape), _whole(bexp.shape),
                  _whole((1, D_SGU))],
        out_specs=[tok(D_NA), pl.BlockSpec((tm // LANES, D_NA, LANES), lambda i: (i, 0, 0)),
                   tok(D_NA), tok(D_NA)],
        out_shape=[out, jax.ShapeDtypeStruct((n // LANES, D_NA, LANES), BF16), out, out],
        compiler_params=_params("parallel"),
        name="in_proj",
    )(x2, g_mix, w_a, w_qv, w_kt, g_sgu_v, wcat, bexp, g_out_a)


ROWS_PER_SLAB = LANES // GRID_W
SLABS_PER_WINDOW = NA_KH // ROWS_PER_SLAB


def _na_body(rows, q_ref, kt_ref, v_ref, bm_ref, o_ref, kk_ref, s_ref, p_ref):
    n_slabs = rows // ROWS_PER_SLAB
    n_tiles = rows // NA_ROW_TILE
    half = LANES // 2

    def shift(s, carry):
        a = kt_ref[s]
        kk_ref[0, s] = a
        kk_ref[1, s] = jnp.concatenate([a[:, half:], kt_ref[s + 1][:, :half]], axis=1)
        return carry
    lax.fori_loop(0, n_slabs - 1, shift, 0, unroll=8)
    last = kt_ref[n_slabs - 1]
    kk_ref[0, n_slabs - 1] = last
    kk_ref[1, n_slabs - 1] = last

    s_ref[1] = jnp.zeros(s_ref.shape[1:], F32)
    p_ref[0] = jnp.zeros(p_ref.shape[1:], BF16)

    lo = lax.broadcasted_iota(jnp.int32, (GRID_W, LANES), 1) < HEAD_DIM
    zero = jnp.zeros((GRID_W, LANES), BF16)

    def first_key_row(r):
        return jnp.clip(r - NA_KH // 2, 0, rows - NA_KH)

    def tok(r):
        return pl.multiple_of(r * GRID_W, GRID_W)

    def step(k, carry):
        slot_a = k % 2
        slot_b = (k + 1) % 2
        tile_a = jnp.minimum(k, n_tiles - 1)
        tile_b = jnp.clip(k - 1, 0, n_tiles - 1)
        tile_c = jnp.clip(k - 2, 0, n_tiles - 1)
        for i in range(NA_ROW_TILE):
            r = tile_c * NA_ROW_TILE + i
            vw = v_ref[0, pl.ds(tok(first_key_row(r)), NA_KH * GRID_W), :]
            o = _dot(p_ref[slot_a, i], vw)
            o_ref[0, pl.ds(tok(r), GRID_W), :] = jnp.where(lo, o[:GRID_W], o[GRID_W:]).astype(BF16)
        for i in range(NA_ROW_TILE):
            r = tile_b * NA_ROW_TILE + i
            s = s_ref[slot_b, i] + bm_ref[0, r - first_key_row(r)]
            e = jnp.exp(s - jnp.max(s, axis=-1, keepdims=True))
            p_ref[slot_b, i] = (e / jnp.sum(e, axis=-1, keepdims=True)).astype(BF16)
        for i in range(NA_ROW_TILE):
            r = tile_a * NA_ROW_TILE + i
            rs = first_key_row(r)
            qrow = q_ref[0, pl.ds(tok(r), GRID_W), :]
            q2 = jnp.concatenate([jnp.where(lo, qrow, zero), jnp.where(lo, zero, qrow)], axis=0)
            odd = rs % ROWS_PER_SLAB
            s0 = rs // ROWS_PER_SLAB
            kw = jnp.concatenate([kk_ref[odd, s0 + j] for j in range(SLABS_PER_WINDOW)], axis=1)
            s_ref[slot_a, i] = _dot(q2, kw)
        return carry

    lax.fori_loop(0, n_tiles + 2, step, 0)


def _na(q, kt, v, bm):
    b, t, _ = q.shape
    rows = t // GRID_W
    assert rows >= 2 * NA_KH and rows % NA_ROW_TILE == 0
    n_pairs = D_NA // LANES
    n_slabs = t // LANES
    seq = pl.BlockSpec((1, t, LANES), lambda bi, p: (bi, 0, p))
    stage = (2, NA_ROW_TILE, 2 * GRID_W, NA_KH * GRID_W)
    return pl.pallas_call(
        functools.partial(_na_body, rows),
        grid=(b, n_pairs),
        in_specs=[seq, pl.BlockSpec((n_slabs, LANES, LANES), lambda bi, p: (bi, p, 0)), seq,
                  pl.BlockSpec((1, NA_KH, 2 * GRID_W, NA_KH * GRID_W), lambda bi, p: (p, 0, 0, 0))],
        out_specs=seq,
        out_shape=jax.ShapeDtypeStruct((b, t, D_NA), BF16),
        scratch_shapes=[pltpu.VMEM((2, n_slabs, LANES, LANES), BF16), pltpu.VMEM(stage, F32),
                        pltpu.VMEM(stage, BF16)],
        compiler_params=_params("parallel", "parallel"),
        name="na",
    )(q, kt, v, bm)


def _na_bias_table(rpb):
    c = np.arange(GRID_W)
    wstart = np.clip(c - NA_KW // 2, 0, GRID_W - NA_KW)
    kc = c[None, :]
    inwin = (kc >= wstart[:, None]) & (kc < wstart[:, None] + NA_KW)
    rel = np.clip(kc - c[:, None], -(NA_KW - 1), NA_KW - 1) + NA_KW - 1
    tab = jnp.where(jnp.asarray(inwin)[None, None], rpb[:, :, rel], NEG_BIG)
    bmh = jnp.stack([tab[:, NA_KH - 1 - d:2 * NA_KH - 1 - d] for d in range(NA_KH)], axis=1)
    h = rpb.shape[0]
    bmh = bmh.transpose(0, 1, 3, 2, 4).reshape(h // 2, 2, NA_KH, GRID_W, NA_KH * GRID_W)
    return bmh.transpose(0, 2, 1, 3, 4).reshape(h // 2, NA_KH, 2 * GRID_W, NA_KH * GRID_W).astype(F32)


def _mix_xattn_body(x_ref, ya_ref, yb_ref, km_ref, vm_ref, gob_ref, wout_ref, gx_ref, wxq_ref,
                    wxo_ref, h2_ref):
    ybn = _rms(yb_ref[0].astype(F32), gob_ref[...]).astype(BF16)
    h1 = x_ref[0] + _dot(ya_ref[0], wout_ref[0:D_SGU, :]) + _dot(ybn, wout_ref[D_SGU:, :])
    hn = _rms(h1, gx_ref[...]).astype(BF16)
    q = _dot(hn, wxq_ref[...]).astype(BF16)
    heads = []
    for h in range(XA_HEADS):
        sl = slice(h * XA_HEAD_DIM, (h + 1) * XA_HEAD_DIM)
        s = _dot_nt(q[:, sl], km_ref[0, :, sl]) * (XA_HEAD_DIM ** -0.5)
        e = jnp.exp(s - jnp.max(s, axis=-1, keepdims=True))
        o = _dot(e.astype(BF16), vm_ref[0, :, sl]) / jnp.sum(e, axis=-1, keepdims=True)
        heads.append(o.astype(BF16))
    h2_ref[0] = h1 + _dot(jnp.concatenate(heads, axis=1), wxo_ref[...])


def _mix_xattn(x, ya, yb, km, vm, g_out_b, w_out, g_xattn, w_xq, w_xo):
    b, t, _ = x.shape
    tm = TOKEN_TILE
    tok = lambda w: pl.BlockSpec((1, tm, w), lambda bi, ti: (bi, ti, 0))
    mem = pl.BlockSpec((1, N_MEM, D_XA), lambda bi, ti: (bi, 0, 0))
    return pl.pallas_call(
        _mix_xattn_body,
        grid=(b, t // tm),
        in_specs=[tok(D_MODEL), tok(D_SGU), tok(D_NA), mem, mem, _whole((1, D_NA)),
                  _whole(w_out.shape), _whole((1, D_MODEL)), _whole(w_xq.shape), _whole(w_xo.shape)],
        out_specs=tok(D_MODEL),
        out_shape=jax.ShapeDtypeStruct((b, t, D_MODEL), F32),
        compiler_params=_params("parallel", "parallel"),
        name="mix_xattn",
    )(x, ya, yb, km, vm, g_out_b, w_out, g_xattn, w_xq, w_xo)


def _route(logits, grp):
    lane = lax.broadcasted_iota(jnp.int32, logits.shape, 1)
    lg = jnp.where(lane < N_GROUPS, logits, NEG_BIG)
    gmax = jnp.max(lg, axis=-1, keepdims=True)
    l_sel = jnp.sum(jnp.where(lane == grp, logits, 0.0), axis=-1, keepdims=True)
    g_top = jnp.exp(l_sel - gmax) / jnp.sum(jnp.exp(lg - gmax), axis=-1, keepdims=True)
    first = N_GROUPS + grp * EXPERTS_PER_GROUP
    le = jnp.where((lane >= first) & (lane < first + EXPERTS_PER_GROUP), logits, NEG_BIG)
    m1 = jnp.max(le, axis=-1, keepdims=True)
    i1 = jnp.min(jnp.where(le == m1, lane, LANES), axis=-1, keepdims=True)
    le2 = jnp.where(lane == i1, NEG_BIG, le)
    m2 = jnp.max(le2, axis=-1, keepdims=True)
    i2 = jnp.min(jnp.where(le2 == m2, lane, LANES), axis=-1, keepdims=True)
    p2 = jnp.exp(m2 - m1)
    w1 = g_top / (1.0 + p2)
    w2 = g_top * p2 / (1.0 + p2)
    return jnp.where(lane == i1, w1, 0.0) + jnp.where(lane == i2, w2, 0.0)


def _split_bf16(x):
    hi = x.astype(BF16)
    return hi, (x - hi.astype(F32)).astype(BF16)


GROUP_ROWS = 8


def _group_body(h_ref, gm_ref, wrt_ref, gid_ref, cnt_ref):
    hn_hi, hn_lo = _split_bf16(_rms(h_ref[...], gm_ref[...]))
    part = _dot_nt(wrt_ref[...], hn_hi) + _dot_nt(wrt_ref[...], hn_lo)
    logit_t = part[:LANES] + part[LANES:]
    row = lax.broadcasted_iota(jnp.int32, (GROUP_ROWS, logit_t.shape[1]), 0)
    lg = jnp.where(row < N_GROUPS, logit_t[:GROUP_ROWS], NEG_BIG)
    gmax = jnp.max(lg, axis=0, keepdims=True)
    gid = jnp.min(jnp.where(lg == gmax, row, GROUP_ROWS), axis=0, keepdims=True)
    gid_ref[0] = gid

    @pl.when(pl.program_id(0) == 0)
    def _():
        cnt_ref[...] = jnp.zeros_like(cnt_ref)
    cnt_ref[...] += jnp.sum((row == gid).astype(F32), axis=1, keepdims=True)


def _group_of_tokens(h2, g_moe, wrt):
    n = h2.shape[0]
    tm = TOKEN_TILE
    return pl.pallas_call(
        _group_body,
        grid=(n // tm,),
        in_specs=[pl.BlockSpec((tm, D_MODEL), lambda i: (i, 0)), _whole((1, D_MODEL)), _whole(wrt.shape)],
        out_specs=[pl.BlockSpec((1, 1, tm), lambda i: (i, 0, 0)), _whole((GROUP_ROWS, LANES))],
        out_shape=[jax.ShapeDtypeStruct((n // tm, 1, tm), jnp.int32),
                   jax.ShapeDtypeStruct((GROUP_ROWS, LANES), F32)],
        compiler_params=_params("arbitrary"),
        name="moe_group",
    )(h2, g_moe, wrt)


def _slot_body(start_ref, gid_ref, before_ref, pos_ref, run_ref):
    @pl.when(pl.program_id(0) == 0)
    def _():
        run_ref[...] = jnp.zeros_like(run_ref)
    gid = gid_ref[0]
    row = lax.broadcasted_iota(jnp.int32, (GROUP_ROWS, gid.shape[1]), 0)
    onehot = row == gid
    earlier = _dot(onehot.astype(BF16), before_ref[...])
    row1 = lax.broadcasted_iota(jnp.int32, (GROUP_ROWS, 1), 0)
    start = jnp.zeros((GROUP_ROWS, 1), F32)
    for g in range(N_GROUPS):
        start = jnp.where(row1 == g, start_ref[g].astype(F32), start)
    slot = jnp.sum(jnp.where(onehot, earlier + (start + run_ref[:, 0:1]), 0.0), axis=0, keepdims=True)
    pos_ref[0] = slot.astype(jnp.int32)
    run_ref[...] += jnp.sum(onehot.astype(F32), axis=1, keepdims=True)


def _sorted_slots(start, gid, before):
    n_tiles, _, tm = gid.shape
    blk = pl.BlockSpec((1, 1, tm), lambda i, *_: (i, 0, 0))
    return pl.pallas_call(
        _slot_body,
        grid_spec=pltpu.PrefetchScalarGridSpec(
            num_scalar_prefetch=1, grid=(n_tiles,),
            in_specs=[blk, pl.BlockSpec(before.shape, lambda i, *_: (0, 0))], out_specs=blk,
            scratch_shapes=[pltpu.VMEM((GROUP_ROWS, LANES), F32)]),
        out_shape=jax.ShapeDtypeStruct(gid.shape, jnp.int32),
        compiler_params=_params("arbitrary"),
        name="moe_slots",
    )(start, gid, before)


N_FILL = N_GROUPS + 1
ROW_DMA_UNROLL = 8


def _move_rows_body(scatter, fill_ref, pos_ref, src_ref, dst_ref, zero_ref, sem):
    tm = pos_ref.shape[2]
    base = pl.program_id(0) * tm

    def row_copy(t):
        here, there = pl.ds(base + t, 1), pl.ds(pos_ref[0, 0, t], 1)
        if scatter:
            return pltpu.make_async_copy(src_ref.at[here], dst_ref.at[there], sem)
        return pltpu.make_async_copy(src_ref.at[there], dst_ref.at[here], sem)

    def start(t, carry):
        row_copy(t).start()
        return carry

    def wait(t, carry):
        row_copy(t).wait()
        return carry

    lax.fori_loop(0, tm, start, 0, unroll=ROW_DMA_UNROLL)

    if scatter:
        @pl.when(pl.program_id(0) == 0)
        def _():
            zero_ref[...] = jnp.zeros_like(zero_ref)
            for f in range(N_FILL):
                def fill_copy(j, f=f):
                    return pltpu.make_async_copy(zero_ref.at[pl.ds(0, 1)],
                                                 dst_ref.at[pl.ds(fill_ref[f] + j, 1)], sem)

                def fill_start(j, carry, fill_copy=fill_copy):
                    fill_copy(j).start()
                    return carry

                def fill_wait(j, carry, fill_copy=fill_copy):
                    fill_copy(j).wait()
                    return carry
                lax.fori_loop(0, fill_ref[N_FILL + f], fill_start, 0)
                lax.fori_loop(0, fill_ref[N_FILL + f], fill_wait, 0)

    lax.fori_loop(0, tm, wait, 0, unroll=ROW_DMA_UNROLL)


def _move_rows(scatter, fill, pos, src, n_out):
    n_tiles, _, tm = pos.shape
    any_space = pl.BlockSpec(memory_space=pl.ANY)
    return pl.pallas_call(
        functools.partial(_move_rows_body, scatter),
        grid_spec=pltpu.PrefetchScalarGridSpec(
            num_scalar_prefetch=1, grid=(n_tiles,),
            in_specs=[pl.BlockSpec((1, 1, tm), lambda i, *_: (i, 0, 0), memory_space=pltpu.SMEM), any_space],
            out_specs=any_space,
            scratch_shapes=[pltpu.VMEM((GROUP_ROWS, D_MODEL), F32), pltpu.SemaphoreType.DMA]),
        out_shape=jax.ShapeDtypeStruct((n_out, D_MODEL), F32),
        compiler_params=_params("arbitrary"),
        name="moe_scatter_rows" if scatter else "moe_gather_rows",
    )(fill, pos, src)


def _moe_body(tile_ref, h_ref, gm_ref, wr_ref, wg_ref, wu_ref, wd_ref, gf_ref, y_ref):
    tm = h_ref.shape[0]
    n_used = tile_ref[0]
    grp = tile_ref[1 + jnp.minimum(pl.program_id(0), n_used - 1)]
    hn_hi, hn_lo = _split_bf16(_rms(h_ref[...], gm_ref[...]))
    part = _dot(hn_hi, wr_ref[...]) + _dot(hn_lo, wr_ref[...])
    gates = _route(part[:, :LANES] + part[:, LANES:], grp)
    lane = lax.broadcasted_iota(jnp.int32, gates.shape, 1)
    first = lax.broadcasted_iota(jnp.int32, (tm, 2 * D_EXPERT), 1) < D_EXPERT
    acts = []
    for j in range(EXPERTS_PER_GROUP // 2):
        e0 = N_GROUPS + grp * EXPERTS_PER_GROUP + 2 * j
        g0 = jnp.sum(jnp.where(lane == e0, gates, 0.0), axis=-1, keepdims=True)
        g1 = jnp.sum(jnp.where(lane == e0 + 1, gates, 0.0), axis=-1, keepdims=True)
        hg = _dot(hn_hi, wg_ref[j])
        hu = _dot(hn_hi, wu_ref[j])
        acts.append((jax.nn.silu(hg) * hu * jnp.where(first, g0, g1)).astype(BF16))
    y_ref[...] = _rms(h_ref[...] + _dot(jnp.concatenate(acts, axis=1), wd_ref[0]), gf_ref[...])


def _moe_sorted(tiles, hs, g_moe, wr, wg2, wu2, wd2, g_final):
    tm = TOKEN_TILE
    n_steps = hs.shape[0] // tm
    ppg = EXPERTS_PER_GROUP // 2
    used = lambda s, tl: jnp.minimum(s, tl[0] - 1)
    grp = lambda s, tl: tl[1 + used(s, tl)]
    const = lambda shape: pl.BlockSpec(shape, lambda s, tl: (0,) * len(shape))
    return pl.pallas_call(
        _moe_body,
        grid_spec=pltpu.PrefetchScalarGridSpec(
            num_scalar_prefetch=1, grid=(n_steps,),
            in_specs=[pl.BlockSpec((tm, D_MODEL), lambda s, tl: (used(s, tl), 0)), const((1, D_MODEL)),
                      const(wr.shape),
                      pl.BlockSpec((ppg, D_MODEL, 2 * D_EXPERT), lambda s, tl: (grp(s, tl), 0, 0)),
                      pl.BlockSpec((ppg, D_MODEL, 2 * D_EXPERT), lambda s, tl: (grp(s, tl), 0, 0)),
                      pl.BlockSpec((1, ppg * 2 * D_EXPERT, D_MODEL), lambda s, tl: (grp(s, tl), 0, 0)),
                      const((1, D_MODEL))],
            out_specs=pl.BlockSpec((tm, D_MODEL), lambda s, tl: (s, 0))),
        out_shape=jax.ShapeDtypeStruct(hs.shape, F32),
        compiler_params=_params("arbitrary"),
        name="moe",
    )(tiles, hs, g_moe, wr, wg2, wu2, wd2, g_final)


def _moe(h2, w):
    n = h2.shape[0]
    tm = TOKEN_TILE
    n_steps = n // tm + N_GROUPS
    gid, cnt = _group_of_tokens(h2, w["g_moe"], w["wrt"])
    cnt = cnt[:N_GROUPS, 0].astype(jnp.int32)
    padded = (cnt + tm - 1) // tm * tm
    end = jnp.cumsum(padded)
    start = end - padded
    tile_group = jnp.minimum(jnp.searchsorted(end // tm, jnp.arange(n_steps), side="right"), N_GROUPS - 1)
    tiles = jnp.concatenate([end[-1:] // tm, tile_group]).astype(jnp.int32)
    fill = jnp.concatenate([start + cnt, end[-1:], padded - cnt, n_steps * tm - end[-1:]]).astype(jnp.int32)
    pos = _sorted_slots(start.astype(jnp.int32), gid, w["before"])
    hs = _move_rows(True, fill, pos, h2, n_steps * tm)
    ys = _moe_sorted(tiles, hs, w["g_moe"], w["wr"], w["wg2"], w["wu2"], w["wd2"], w["g_final"])
    return _move_rows(False, jnp.zeros((2 * N_FILL,), jnp.int32), pos, ys, n)


def _prep_weights(g_mix, w_in, g_sgu_v, sgu_w, sgu_b, na_rpb, g_out_a, g_out_b, w_out, g_xattn, g_mem,
                  w_xq, w_xkv, w_xo, g_moe, w_router_group, w_router_expert, w_exp_gate, w_exp_up,
                  w_exp_down, g_final):
    row = lambda g: g.reshape(1, -1).astype(F32)
    heads = sgu_w.shape[0]
    wcat = jnp.concatenate([sgu_w[0::2], sgu_w[1::2]], axis=2).astype(BF16)
    bexp = jnp.repeat(sgu_b.T, HEAD_DIM, axis=1).astype(F32)
    assert heads * HEAD_DIM == D_SGU
    wr = jnp.zeros((D_MODEL, LANES), F32)
    wr = wr.at[:, :N_GROUPS].set(w_router_group).at[:, N_GROUPS:N_GROUPS + N_EXPERTS].set(w_router_expert)
    wr_hi = wr.astype(BF16)
    wr_split = jnp.concatenate([wr_hi, (wr - wr_hi.astype(F32)).astype(BF16)], axis=1)
    before = jnp.triu(jnp.ones((TOKEN_TILE, TOKEN_TILE), BF16), k=1)
    pair_cols = lambda w: (w.reshape(N_EXPERTS // 2, 2, D_MODEL, D_EXPERT).transpose(0, 2, 1, 3)
                           .reshape(N_EXPERTS // 2, D_MODEL, 2 * D_EXPERT).astype(BF16))
    return dict(
        g_mix=row(g_mix), w_a=w_in[:, :2 * D_SGU].astype(BF16),
        w_qv=jnp.concatenate([w_in[:, 2 * D_SGU:2 * D_SGU + D_NA], w_in[:, 2 * D_SGU + 2 * D_NA:]], axis=1).astype(BF16),
        w_kt=w_in[:, 2 * D_SGU + D_NA:2 * D_SGU + 2 * D_NA].T.astype(BF16),
        g_sgu_v=row(g_sgu_v), wcat=wcat, bexp=bexp,
        bm=_na_bias_table(na_rpb), g_out_a=row(g_out_a), g_out_b=row(g_out_b), w_out=w_out.astype(BF16),
        g_xattn=row(g_xattn), g_mem=row(g_mem), w_xq=w_xq.astype(BF16), w_xkv=w_xkv.astype(BF16),
        w_xo=w_xo.astype(BF16), g_moe=row(g_moe), wr=wr_split, wrt=wr_split.T, before=before,
        wg2=pair_cols(w_exp_gate), wu2=pair_cols(w_exp_up),
        wd2=w_exp_down.reshape(N_GROUPS, EXPERTS_PER_GROUP * D_EXPERT, D_MODEL).astype(BF16),
        g_final=row(g_final))


def _trunk(x, mem, w):
    b, t, _ = x.shape
    km, vm = _mem_kv(mem, w["g_mem"], w["w_xkv"])
    q, kt, v, ya = _in_proj(x.reshape(b * t, D_MODEL), w["g_mix"], w["w_a"], w["w_qv"], w["w_kt"],
                            w["g_sgu_v"], w["wcat"], w["bexp"], w["g_out_a"])
    seq = lambda a: a.reshape(b, t, -1)
    yb = _na(seq(q), kt, seq(v), w["bm"])
    h2 = _mix_xattn(x, seq(ya), yb, km, vm, w["g_out_b"], w["w_out"], w["g_xattn"], w["w_xq"], w["w_xo"])
    y = _moe(h2.reshape(b * t, D_MODEL), w)
    return y.reshape(b, t, D_MODEL)


def kernel(x_prompt, x_sample, mem_prompt, mem_sample, g_mix, w_in, g_sgu_v, sgu_w, sgu_b, na_rpb, g_out_a,
           g_out_b, w_out, g_xattn, g_mem, w_xq, w_xkv, w_xo, g_moe, w_router_group, w_router_expert,
           w_exp_gate, w_exp_up, w_exp_down, g_final):
    assert g_mix.shape[0] == 1
    w = _prep_weights(g_mix[0], w_in[0], g_sgu_v[0], sgu_w[0], sgu_b[0], na_rpb[0], g_out_a[0], g_out_b[0],
                      w_out[0], g_xattn[0], g_mem[0], w_xq[0], w_xkv[0], w_xo[0], g_moe[0],
                      w_router_group[0], w_router_expert[0], w_exp_gate[0], w_exp_up[0], w_exp_down[0],
                      g_final)
    return (_trunk(x_prompt, mem_prompt, w), _trunk(x_sample, mem_sample, w))
```

```python
import functools

import jax
import jax.numpy as jnp
import numpy as np
from jax import lax
from jax.experimental import pallas as pl
from jax.experimental.pallas import tpu as pltpu

F32 = jnp.float32
BF16 = jnp.bfloat16

D_MODEL = 1024
N_MEM = 256
GRID_W = 64
HEAD_DIM = 64
D_SGU = 512
D_NA = 512
CHUNK = 128
NA_KH = 8
NA_KW = 16
XA_HEADS = 4
XA_HEAD_DIM = 128
D_XA = 512
N_GROUPS = 4
EXPERTS_PER_GROUP = 8
N_EXPERTS = 32
D_EXPERT = 128
EPS = 1e-6

LANES = 128
NEG_BIG = -1e30
TOKEN_TILE = 1024
NA_ROW_TILE = 8
VMEM_LIMIT_BYTES = 56 * 1024 * 1024


def _rms(x, g):
    return x * lax.rsqrt(jnp.mean(x * x, axis=-1, keepdims=True) + EPS) * g


def _dot(a, b):
    return jnp.dot(a, b, preferred_element_type=F32)


def _dot_nt(a, b):
    return lax.dot_general(a, b, (((1,), (1,)), ((), ())), preferred_element_type=F32)


def _whole(shape):
    nd = len(shape)
    return pl.BlockSpec(shape, lambda *_: (0,) * nd)


def _params(*sem, flags=None):
    return pltpu.CompilerParams(dimension_semantics=sem, vmem_limit_bytes=VMEM_LIMIT_BYTES, flags=flags)


def _mem_kv_body(mem_ref, g_ref, w_ref, k_ref, v_ref):
    memn = _rms(mem_ref[0], g_ref[...]).astype(BF16)
    kv = _dot(memn, w_ref[...])
    k_ref[0] = kv[:, :D_XA].astype(BF16)
    v_ref[0] = kv[:, D_XA:].astype(BF16)


def _mem_kv(mem, g_mem, w_xkv):
    b = mem.shape[0]
    out = jax.ShapeDtypeStruct((b, N_MEM, D_XA), BF16)
    return pl.pallas_call(
        _mem_kv_body,
        grid=(b,),
        in_specs=[pl.BlockSpec((1, N_MEM, D_MODEL), lambda i: (i, 0, 0)),
                  _whole((1, D_MODEL)), _whole((D_MODEL, 2 * D_XA))],
        out_specs=[pl.BlockSpec((1, N_MEM, D_XA), lambda i: (i, 0, 0))] * 2,
        out_shape=[out, out],
        compiler_params=_params("parallel"),
        name="mem_kv",
    )(mem, g_mem, w_xkv)


def _in_proj_body(x_ref, gmix_ref, wa_ref, wqv_ref, wkt_ref, gv_ref, wcat_ref, bexp_ref, goa_ref,
                  q_ref, kt_ref, v_ref, ya_ref):
    tm = x_ref.shape[0]
    hn = _rms(x_ref[...], gmix_ref[...]).astype(BF16)
    a = jax.nn.gelu(_dot(hn, wa_ref[...]))
    u = a[:, :D_SGU]
    vn = _rms(a[:, D_SGU:], gv_ref[...]).astype(BF16)
    lo = lax.broadcasted_iota(jnp.int32, (CHUNK, LANES), 1) < HEAD_DIM
    zero = jnp.zeros((CHUNK, LANES), BF16)
    chunks = []
    for c in range(tm // CHUNK):
        pairs = []
        for p in range(D_SGU // LANES):
            vp = vn[c * CHUNK:(c + 1) * CHUNK, p * LANES:(p + 1) * LANES]
            rhs = jnp.concatenate([jnp.where(lo, vp, zero), jnp.where(lo, zero, vp)], axis=0)
            pairs.append(_dot(wcat_ref[p], rhs))
        chunks.append(jnp.concatenate(pairs, axis=1) + bexp_ref[...])
    ya = u * jnp.concatenate(chunks, axis=0)
    ya_ref[...] = _rms(ya, goa_ref[...]).astype(BF16)
    qv = _dot(hn, wqv_ref[...])
    q_ref[...] = (qv[:, 0:D_NA] * (HEAD_DIM ** -0.5)).astype(BF16)
    v_ref[...] = qv[:, D_NA:].astype(BF16)
    kt = _dot_nt(wkt_ref[...], hn).astype(BF16)
    for c in range(tm // LANES):
        kt_ref[c] = kt[:, c * LANES:(c + 1) * LANES]


def _in_proj(x2, g_mix, w_a, w_qv, w_kt, g_sgu_v, wcat, bexp, g_out_a):
    n = x2.shape[0]
    tm = TOKEN_TILE
    out = jax.ShapeDtypeStruct((n, D_NA), BF16)
    tok = lambda w: pl.BlockSpec((tm, w), lambda i: (i, 0))
    return pl.pallas_call(
        _in_proj_body,
        grid=(n // tm,),
        in_specs=[tok(D_MODEL), _whole((1, D_MODEL)), _whole(w_a.shape), _whole(w_qv.shape),
                  _whole(w_kt.shape), _whole((1, D_SGU)), _whole(wcat.shape), _whole(bexp.shape),
                  _whole((1, D_SGU))],
        out_specs=[tok(D_NA), pl.BlockSpec((tm // LANES, D_NA, LANES), lambda i: (i, 0, 0)),
                   tok(D_NA), tok(D_NA)],
        out_shape=[out, jax.ShapeDtypeStruct((n // LANES, D_NA, LANES), BF16), out, out],
        compiler_params=_params("parallel"),
        name="in_proj",
    )(x2, g_mix, w_a, w_qv, w_kt, g_sgu_v, wcat, bexp, g_out_a)


ROWS_PER_SLAB = LANES // GRID_W
SLABS_PER_WINDOW = NA_KH // ROWS_PER_SLAB


def _na_body(rows, q_ref, kt_ref, v_ref, bm_ref, o_ref, kk_ref, s_ref, p_ref):
    n_slabs = rows // ROWS_PER_SLAB
    n_tiles = rows // NA_ROW_TILE
    half = LANES // 2

    def shift(s, carry):
        a = kt_ref[s]
        kk_ref[0, s] = a
        kk_ref[1, s] = jnp.concatenate([a[:, half:], kt_ref[s + 1][:, :half]], axis=1)
        return carry
    lax.fori_loop(0, n_slabs - 1, shift, 0, unroll=8)
    last = kt_ref[n_slabs - 1]
    kk_ref[0, n_slabs - 1] = last
    kk_ref[1, n_slabs - 1] = last

    s_ref[1] = jnp.zeros(s_ref.shape[1:], F32)
    p_ref[0] = jnp.zeros(p_ref.shape[1:], BF16)

    lo = lax.broadcasted_iota(jnp.int32, (GRID_W, LANES), 1) < HEAD_DIM
    zero = jnp.zeros((GRID_W, LANES), BF16)

    def first_key_row(r):
        return jnp.clip(r - NA_KH // 2, 0, rows - NA_KH)

    def tok(r):
        return pl.multiple_of(r * GRID_W, GRID_W)

    def step(k, carry):
        slot_a = k % 2
        slot_b = (k + 1) % 2
        tile_a = jnp.minimum(k, n_tiles - 1)
        tile_b = jnp.clip(k - 1, 0, n_tiles - 1)
        tile_c = jnp.clip(k - 2, 0, n_tiles - 1)
        for i in range(NA_ROW_TILE):
            r = tile_c * NA_ROW_TILE + i
            vw = v_ref[0, pl.ds(tok(first_key_row(r)), NA_KH * GRID_W), :]
            o = _dot(p_ref[slot_a, i], vw)
            o_ref[0, pl.ds(tok(r), GRID_W), :] = jnp.where(lo, o[:GRID_W], o[GRID_W:]).astype(BF16)
        for i in range(NA_ROW_TILE):
            r = tile_b * NA_ROW_TILE + i
            s = s_ref[slot_b, i] + bm_ref[0, r - first_key_row(r)]
            e = jnp.exp(s - jnp.max(s, axis=-1, keepdims=True))
            p_ref[slot_b, i] = (e / jnp.sum(e, axis=-1, keepdims=True)).astype(BF16)
        for i in range(NA_ROW_TILE):
            r = tile_a * NA_ROW_TILE + i
            rs = first_key_row(r)
            qrow = q_ref[0, pl.ds(tok(r), GRID_W), :]
            q2 = jnp.concatenate([jnp.where(lo, qrow, zero), jnp.where(lo, zero, qrow)], axis=0)
            odd = rs % ROWS_PER_SLAB
            s0 = rs // ROWS_PER_SLAB
            kw = jnp.concatenate([kk_ref[odd, s0 + j] for j in range(SLABS_PER_WINDOW)], axis=1)
            s_ref[slot_a, i] = _dot(q2, kw)
        return carry

    lax.fori_loop(0, n_tiles + 2, step, 0)


def _na(q, kt, v, bm):
    b, t, _ = q.shape
    rows = t // GRID_W
    assert rows >= 2 * NA_KH and rows % NA_ROW_TILE == 0
    n_pairs = D_NA // LANES
    n_slabs = t // LANES
    seq = pl.BlockSpec((1, t, LANES), lambda bi, p: (bi, 0, p))
    stage = (2, NA_ROW_TILE, 2 * GRID_W, NA_KH * GRID_W)
    return pl.pallas_call(
        functools.partial(_na_body, rows),
        grid=(b, n_pairs),
        in_specs=[seq, pl.BlockSpec((n_slabs, LANES, LANES), lambda bi, p: (bi, p, 0)), seq,
                  pl.BlockSpec((1, NA_KH, 2 * GRID_W, NA_KH * GRID_W), lambda bi, p: (p, 0, 0, 0))],
        out_specs=seq,
        out_shape=jax.ShapeDtypeStruct((b, t, D_NA), BF16),
        scratch_shapes=[pltpu.VMEM((2, n_slabs, LANES, LANES), BF16), pltpu.VMEM(stage, F32),
                        pltpu.VMEM(stage, BF16)],
        compiler_params=_params("parallel", "parallel"),
        name="na",
    )(q, kt, v, bm)


def _na_bias_table(rpb):
    c = np.arange(GRID_W)
    wstart = np.clip(c - NA_KW // 2, 0, GRID_W - NA_KW)
    kc = c[None, :]
    inwin = (kc >= wstart[:, None]) & (kc < wstart[:, None] + NA_KW)
    rel = np.clip(kc - c[:, None], -(NA_KW - 1), NA_KW - 1) + NA_KW - 1
    tab = jnp.where(jnp.asarray(inwin)[None, None], rpb[:, :, rel], NEG_BIG)
    bmh = jnp.stack([tab[:, NA_KH - 1 - d:2 * NA_KH - 1 - d] for d in range(NA_KH)], axis=1)
    h = rpb.shape[0]
    bmh = bmh.transpose(0, 1, 3, 2, 4).reshape(h // 2, 2, NA_KH, GRID_W, NA_KH * GRID_W)
    return bmh.transpose(0, 2, 1, 3, 4).reshape(h // 2, NA_KH, 2 * GRID_W, NA_KH * GRID_W).astype(F32)


def _mix_xattn_body(x_ref, ya_ref, yb_ref, km_ref, vm_ref, gob_ref, wout_ref, gx_ref, wxq_ref,
                    wxo_ref, h2_ref):
    ybn = _rms(yb_ref[0].astype(F32), gob_ref[...]).astype(BF16)
    h1 = x_ref[0] + _dot(ya_ref[0], wout_ref[0:D_SGU, :]) + _dot(ybn, wout_ref[D_SGU:, :])
    hn = _rms(h1, gx_ref[...]).astype(BF16)
    q = _dot(hn, wxq_ref[...]).astype(BF16)
    heads = []
    for h in range(XA_HEADS):
        sl = slice(h * XA_HEAD_DIM, (h + 1) * XA_HEAD_DIM)
        s = _dot_nt(q[:, sl], km_ref[0, :, sl]) * (XA_HEAD_DIM ** -0.5)
        e = jnp.exp(s - jnp.max(s, axis=-1, keepdims=True))
        o = _dot(e.astype(BF16), vm_ref[0, :, sl]) / jnp.sum(e, axis=-1, keepdims=True)
        heads.append(o.astype(BF16))
    h2_ref[0] = h1 + _dot(jnp.concatenate(heads, axis=1), wxo_ref[...])


def _mix_xattn(x, ya, yb, km, vm, g_out_b, w_out, g_xattn, w_xq, w_xo):
    b, t, _ = x.shape
    tm = TOKEN_TILE
    tok = lambda w: pl.BlockSpec((1, tm, w), lambda bi, ti: (bi, ti, 0))
    mem = pl.BlockSpec((1, N_MEM, D_XA), lambda bi, ti: (bi, 0, 0))
    return pl.pallas_call(
        _mix_xattn_body,
        grid=(b, t // tm),
        in_specs=[tok(D_MODEL), tok(D_SGU), tok(D_NA), mem, mem, _whole((1, D_NA)),
                  _whole(w_out.shape), _whole((1, D_MODEL)), _whole(w_xq.shape), _whole(w_xo.shape)],
        out_specs=tok(D_MODEL),
        out_shape=jax.ShapeDtypeStruct((b, t, D_MODEL), F32),
        compiler_params=_params("parallel", "parallel"),
        name="mix_xattn",
    )(x, ya, yb, km, vm, g_out_b, w_out, g_xattn, w_xq, w_xo)


def _route(logits, grp):
    lane = lax.broadcasted_iota(jnp.int32, logits.shape, 1)
    lg = jnp.where(lane < N_GROUPS, logits, NEG_BIG)
    gmax = jnp.max(lg, axis=-1, keepdims=True)
    l_sel = jnp.sum(jnp.where(lane == grp, logits, 0.0), axis=-1, keepdims=True)
    g_top = jnp.exp(l_sel - gmax) / jnp.sum(jnp.exp(lg - gmax), axis=-1, keepdims=True)
    first = N_GROUPS + grp * EXPERTS_PER_GROUP
    le = jnp.where((lane >= first) & (lane < first + EXPERTS_PER_GROUP), logits, NEG_BIG)
    m1 = jnp.max(le, axis=-1, keepdims=True)
    i1 = jnp.min(jnp.where(le == m1, lane, LANES), axis=-1, keepdims=True)
    le2 = jnp.where(lane == i1, NEG_BIG, le)
    m2 = jnp.max(le2, axis=-1, keepdims=True)
    i2 = jnp.min(jnp.where(le2 == m2, lane, LANES), axis=-1, keepdims=True)
    p2 = jnp.exp(m2 - m1)
    w1 = g_top / (1.0 + p2)
    w2 = g_top * p2 / (1.0 + p2)
    return jnp.where(lane == i1, w1, 0.0) + jnp.where(lane == i2, w2, 0.0)


def _split_bf16(x):
    hi = x.astype(BF16)
    return hi, (x - hi.astype(F32)).astype(BF16)


GROUP_ROWS = 8


def _group_body(h_ref, gm_ref, wrt_ref, gid_ref, cnt_ref):
    hn_hi, hn_lo = _split_bf16(_rms(h_ref[...], gm_ref[...]))
    part = _dot_nt(wrt_ref[...], hn_hi) + _dot_nt(wrt_ref[...], hn_lo)
    logit_t = part[:LANES] + part[LANES:]
    row = lax.broadcasted_iota(jnp.int32, (GROUP_ROWS, logit_t.shape[1]), 0)
    lg = jnp.where(row < N_GROUPS, logit_t[:GROUP_ROWS], NEG_BIG)
    gmax = jnp.max(lg, axis=0, keepdims=True)
    gid = jnp.min(jnp.where(lg == gmax, row, GROUP_ROWS), axis=0, keepdims=True)
    gid_ref[0] = gid

    @pl.when(pl.program_id(0) == 0)
    def _():
        cnt_ref[...] = jnp.zeros_like(cnt_ref)
    cnt_ref[...] += jnp.sum((row == gid).astype(F32), axis=1, keepdims=True)


def _group_of_tokens(h2, g_moe, wrt):
    n = h2.shape[0]
    tm = TOKEN_TILE
    return pl.pallas_call(
        _group_body,
        grid=(n // tm,),
        in_specs=[pl.BlockSpec((tm, D_MODEL), lambda i: (i, 0)), _whole((1, D_MODEL)), _whole(wrt.shape)],
        out_specs=[pl.BlockSpec((1, 1, tm), lambda i: (i, 0, 0)), _whole((GROUP_ROWS, LANES))],
        out_shape=[jax.ShapeDtypeStruct((n // tm, 1, tm), jnp.int32),
                   jax.ShapeDtypeStruct((GROUP_ROWS, LANES), F32)],
        compiler_params=_params("arbitrary"),
        name="moe_group",
    )(h2, g_moe, wrt)


def _slot_body(start_ref, gid_ref, before_ref, pos_ref, run_ref):
    @pl.when(pl.program_id(0) == 0)
    def _():
        run_ref[...] = jnp.zeros_like(run_ref)
    gid = gid_ref[0]
    row = lax.broadcasted_iota(jnp.int32, (GROUP_ROWS, gid.shape[1]), 0)
    onehot = row == gid
    earlier = _dot(onehot.astype(BF16), before_ref[...])
    row1 = lax.broadcasted_iota(jnp.int32, (GROUP_ROWS, 1), 0)
    start = jnp.zeros((GROUP_ROWS, 1), F32)
    for g in range(N_GROUPS):
        start = jnp.where(row1 == g, start_ref[g].astype(F32), start)
    slot = jnp.sum(jnp.where(onehot, earlier + (start + run_ref[:, 0:1]), 0.0), axis=0, keepdims=True)
    pos_ref[0] = slot.astype(jnp.int32)
    run_ref[...] += jnp.sum(onehot.astype(F32), axis=1, keepdims=True)


def _sorted_slots(start, gid, before):
    n_tiles, _, tm = gid.shape
    blk = pl.BlockSpec((1, 1, tm), lambda i, *_: (i, 0, 0))
    return pl.pallas_call(
        _slot_body,
        grid_spec=pltpu.PrefetchScalarGridSpec(
            num_scalar_prefetch=1, grid=(n_tiles,),
            in_specs=[blk, pl.BlockSpec(before.shape, lambda i, *_: (0, 0))], out_specs=blk,
            scratch_shapes=[pltpu.VMEM((GROUP_ROWS, LANES), F32)]),
        out_shape=jax.ShapeDtypeStruct(gid.shape, jnp.int32),
        compiler_params=_params("arbitrary"),
        name="moe_slots",
    )(start, gid, before)


N_FILL = N_GROUPS + 1
ROW_DMA_UNROLL = 8
SLAB = 8
assert SLAB * LANES == D_MODEL


def _rows_to_slabs(x, slab_ref):
    tm = x.shape[0]
    for k in range(SLAB):
        slab_ref[pl.ds(k, tm, stride=SLAB), :] = x[:, k * LANES:(k + 1) * LANES]


def _slabs_to_rows(slab_ref, tm):
    return jnp.concatenate([slab_ref[pl.ds(k, tm, stride=SLAB), :] for k in range(SLAB)], axis=1)


def _tile_of(r):
    return pl.ds(pl.multiple_of(r * SLAB, SLAB), SLAB)


def _scatter_body(fill_ref, pos_ref, h_ref, hs_ref, slab_ref, zero_ref, sem):
    tm = h_ref.shape[0]
    _rows_to_slabs(h_ref[...], slab_ref)

    def row_copy(t):
        return pltpu.make_async_copy(slab_ref.at[_tile_of(t)], hs_ref.at[_tile_of(pos_ref[0, 0, t])], sem)

    def start(t, carry):
        row_copy(t).start()
        return carry

    def wait(t, carry):
        row_copy(t).wait()
        return carry

    lax.fori_loop(0, tm, start, 0, unroll=ROW_DMA_UNROLL)

    @pl.when(pl.program_id(0) == 0)
    def _():
        zero_ref[...] = jnp.zeros_like(zero_ref)
        for f in range(N_FILL):
            def fill_copy(j, f=f):
                return pltpu.make_async_copy(zero_ref, hs_ref.at[_tile_of(fill_ref[f] + j)], sem)

            def fill_start(j, carry, fill_copy=fill_copy):
                fill_copy(j).start()
                return carry

            def fill_wait(j, carry, fill_copy=fill_copy):
                fill_copy(j).wait()
                return carry
            lax.fori_loop(0, fill_ref[N_FILL + f], fill_start, 0)
            lax.fori_loop(0, fill_ref[N_FILL + f], fill_wait, 0)

    lax.fori_loop(0, tm, wait, 0, unroll=ROW_DMA_UNROLL)


def _scatter_rows(fill, pos, h2, n_out):
    n_tiles, _, tm = pos.shape
    return pl.pallas_call(
        _scatter_body,
        grid_spec=pltpu.PrefetchScalarGridSpec(
            num_scalar_prefetch=1, grid=(n_tiles,),
            in_specs=[pl.BlockSpec((1, 1, tm), lambda i, *_: (i, 0, 0), memory_space=pltpu.SMEM),
                      pl.BlockSpec((tm, D_MODEL), lambda i, *_: (i, 0))],
            out_specs=pl.BlockSpec(memory_space=pl.ANY),
            scratch_shapes=[pltpu.VMEM((tm * SLAB, LANES), F32), pltpu.VMEM((SLAB, LANES), F32),
                            pltpu.SemaphoreType.DMA]),
        out_shape=jax.ShapeDtypeStruct((n_out * SLAB, LANES), F32),
        compiler_params=_params("arbitrary"),
        name="moe_scatter_rows",
    )(fill, pos, h2)


def _gather_body(pos_ref, ys_ref, y_ref, slab_ref, sem):
    tm = y_ref.shape[0]

    def row_copy(t):
        return pltpu.make_async_copy(ys_ref.at[_tile_of(pos_ref[0, 0, t])], slab_ref.at[_tile_of(t)], sem)

    def start(t, carry):
        row_copy(t).start()
        return carry

    def wait(t, carry):
        row_copy(t).wait()
        return carry

    lax.fori_loop(0, tm, start, 0, unroll=ROW_DMA_UNROLL)
    lax.fori_loop(0, tm, wait, 0, unroll=ROW_DMA_UNROLL)
    y_ref[...] = _slabs_to_rows(slab_ref, tm)


def _gather_rows(pos, ys, n):
    n_tiles, _, tm = pos.shape
    return pl.pallas_call(
        _gather_body,
        grid=(n_tiles,),
        in_specs=[pl.BlockSpec((1, 1, tm), lambda i: (i, 0, 0), memory_space=pltpu.SMEM),
                  pl.BlockSpec(memory_space=pl.ANY)],
        out_specs=pl.BlockSpec((tm, D_MODEL), lambda i: (i, 0)),
        out_shape=jax.ShapeDtypeStruct((n, D_MODEL), F32),
        scratch_shapes=[pltpu.VMEM((tm * SLAB, LANES), F32), pltpu.SemaphoreType.DMA],
        compiler_params=_params("arbitrary"),
        name="moe_gather_rows",
    )(pos, ys)


def _moe_body(tile_ref, h_ref, gm_ref, wr_ref, wg_ref, wu_ref, wd_ref, gf_ref, y_ref):
    tm = h_ref.shape[0] // SLAB
    n_used = tile_ref[0]
    grp = tile_ref[1 + jnp.minimum(pl.program_id(0), n_used - 1)]
    h = _slabs_to_rows(h_ref, tm)
    hn_hi, hn_lo = _split_bf16(_rms(h, gm_ref[...]))
    part = _dot(hn_hi, wr_ref[...]) + _dot(hn_lo, wr_ref[...])
    gates = _route(part[:, :LANES] + part[:, LANES:], grp)
    lane = lax.broadcasted_iota(jnp.int32, gates.shape, 1)
    first = lax.broadcasted_iota(jnp.int32, (tm, 2 * D_EXPERT), 1) < D_EXPERT
    acts = []
    for j in range(EXPERTS_PER_GROUP // 2):
        e0 = N_GROUPS + grp * EXPERTS_PER_GROUP + 2 * j
        g0 = jnp.sum(jnp.where(lane == e0, gates, 0.0), axis=-1, keepdims=True)
        g1 = jnp.sum(jnp.where(lane == e0 + 1, gates, 0.0), axis=-1, keepdims=True)
        hg = _dot(hn_hi, wg_ref[j])
        hu = _dot(hn_hi, wu_ref[j])
        acts.append((jax.nn.silu(hg) * hu * jnp.where(first, g0, g1)).astype(BF16))
    y = _rms(_slabs_to_rows(h_ref, tm) + _dot(jnp.concatenate(acts, axis=1), wd_ref[0]), gf_ref[...])
    _rows_to_slabs(y, y_ref)


def _moe_sorted(tiles, hs, g_moe, wr, wg2, wu2, wd2, g_final):
    blk = TOKEN_TILE * SLAB
    n_steps = hs.shape[0] // blk
    ppg = EXPERTS_PER_GROUP // 2
    used = lambda s, tl: jnp.minimum(s, tl[0] - 1)
    grp = lambda s, tl: tl[1 + used(s, tl)]
    const = lambda shape: pl.BlockSpec(shape, lambda s, tl: (0,) * len(shape))
    return pl.pallas_call(
        _moe_body,
        grid_spec=pltpu.PrefetchScalarGridSpec(
            num_scalar_prefetch=1, grid=(n_steps,),
            in_specs=[pl.BlockSpec((blk, LANES), lambda s, tl: (used(s, tl), 0)), const((1, D_MODEL)),
                      const(wr.shape),
                      pl.BlockSpec((ppg, D_MODEL, 2 * D_EXPERT), lambda s, tl: (grp(s, tl), 0, 0)),
                      pl.BlockSpec((ppg, D_MODEL, 2 * D_EXPERT), lambda s, tl: (grp(s, tl), 0, 0)),
                      pl.BlockSpec((1, ppg * 2 * D_EXPERT, D_MODEL), lambda s, tl: (grp(s, tl), 0, 0)),
                      const((1, D_MODEL))],
            out_specs=pl.BlockSpec((blk, LANES), lambda s, tl: (s, 0))),
        out_shape=jax.ShapeDtypeStruct(hs.shape, F32),
        compiler_params=_params("arbitrary"),
        name="moe",
    )(tiles, hs, g_moe, wr, wg2, wu2, wd2, g_final)


def _moe(h2, w):
    n = h2.shape[0]
    tm = TOKEN_TILE
    n_steps = n // tm + N_GROUPS
    gid, cnt = _group_of_tokens(h2, w["g_moe"], w["wrt"])
    cnt = cnt[:N_GROUPS, 0].astype(jnp.int32)
    padded = (cnt + tm - 1) // tm * tm
    end = jnp.cumsum(padded)
    start = end - padded
    tile_group = jnp.sum(jnp.arange(n_steps)[:, None] >= (end // tm)[None, :], axis=1)
    tiles = jnp.concatenate([end[-1:] // tm, jnp.minimum(tile_group, N_GROUPS - 1)]).astype(jnp.int32)
    fill = jnp.concatenate([start + cnt, end[-1:], padded - cnt, n_steps * tm - end[-1:]]).astype(jnp.int32)
    pos = _sorted_slots(start.astype(jnp.int32), gid, w["before"])
    hs = _scatter_rows(fill, pos, h2, n_steps * tm)
    ys = _moe_sorted(tiles, hs, w["g_moe"], w["wr"], w["wg2"], w["wu2"], w["wd2"], w["g_final"])
    return _gather_rows(pos, ys, n)


def _prep_weights(g_mix, w_in, g_sgu_v, sgu_w, sgu_b, na_rpb, g_out_a, g_out_b, w_out, g_xattn, g_mem,
                  w_xq, w_xkv, w_xo, g_moe, w_router_group, w_router_expert, w_exp_gate, w_exp_up,
                  w_exp_down, g_final):
    row = lambda g: g.reshape(1, -1).astype(F32)
    heads = sgu_w.shape[0]
    wcat = jnp.concatenate([sgu_w[0::2], sgu_w[1::2]], axis=2).astype(BF16)
    bexp = jnp.repeat(sgu_b.T, HEAD_DIM, axis=1).astype(F32)
    assert heads * HEAD_DIM == D_SGU
    wr = jnp.zeros((D_MODEL, LANES), F32)
    wr = wr.at[:, :N_GROUPS].set(w_router_group).at[:, N_GROUPS:N_GROUPS + N_EXPERTS].set(w_router_expert)
    wr_hi = wr.astype(BF16)
    wr_split = jnp.concatenate([wr_hi, (wr - wr_hi.astype(F32)).astype(BF16)], axis=1)
    before = jnp.triu(jnp.ones((TOKEN_TILE, TOKEN_TILE), BF16), k=1)
    pair_cols = lambda w: (w.reshape(N_EXPERTS // 2, 2, D_MODEL, D_EXPERT).transpose(0, 2, 1, 3)
                           .reshape(N_EXPERTS // 2, D_MODEL, 2 * D_EXPERT).astype(BF16))
    return dict(
        g_mix=row(g_mix), w_a=w_in[:, :2 * D_SGU].astype(BF16),
        w_qv=jnp.concatenate([w_in[:, 2 * D_SGU:2 * D_SGU + D_NA], w_in[:, 2 * D_SGU + 2 * D_NA:]], axis=1).astype(BF16),
        w_kt=w_in[:, 2 * D_SGU + D_NA:2 * D_SGU + 2 * D_NA].T.astype(BF16),
        g_sgu_v=row(g_sgu_v), wcat=wcat, bexp=bexp,
        bm=_na_bias_table(na_rpb), g_out_a=row(g_out_a), g_out_b=row(g_out_b), w_out=w_out.astype(BF16),
        g_xattn=row(g_xattn), g_mem=row(g_mem), w_xq=w_xq.astype(BF16), w_xkv=w_xkv.astype(BF16),
        w_xo=w_xo.astype(BF16), g_moe=row(g_moe), wr=wr_split, wrt=wr_split.T, before=before,
        wg2=pair_cols(w_exp_gate), wu2=pair_cols(w_exp_up),
        wd2=w_exp_down.reshape(N_GROUPS, EXPERTS_PER_GROUP * D_EXPERT, D_MODEL).astype(BF16),
        g_final=row(g_final))


def _trunk(x, mem, w):
    b, t, _ = x.shape
    km, vm = _mem_kv(mem, w["g_mem"], w["w_xkv"])
    q, kt, v, ya = _in_proj(x.reshape(b * t, D_MODEL), w["g_mix"], w["w_a"], w["w_qv"], w["w_kt"],
                            w["g_sgu_v"], w["wcat"], w["bexp"], w["g_out_a"])
    seq = lambda a: a.reshape(b, t, -1)
    yb = _na(seq(q), kt, seq(v), w["bm"])
    h2 = _mix_xattn(x, seq(ya), yb, km, vm, w["g_out_b"], w["w_out"], w["g_xattn"], w["w_xq"], w["w_xo"])
    y = _moe(h2.reshape(b * t, D_MODEL), w)
    return y.reshape(b, t, D_MODEL)


def kernel(x_prompt, x_sample, mem_prompt, mem_sample, g_mix, w_in, g_sgu_v, sgu_w, sgu_b, na_rpb, g_out_a,
           g_out_b, w_out, g_xattn, g_mem, w_xq, w_xkv, w_xo, g_moe, w_router_group, w_router_expert,
           w_exp_gate, w_exp_up, w_exp_down, g_final):
    assert g_mix.shape[0] == 1
    w = _prep_weights(g_mix[0], w_in[0], g_sgu_v[0], sgu_w[0], sgu_b[0], na_rpb[0], g_out_a[0], g_out_b[0],
                      w_out[0], g_xattn[0], g_mem[0], w_xq[0], w_xkv[0], w_xo[0], g_moe[0],
                      w_router_group[0], w_router_expert[0], w_exp_gate[0], w_exp_up[0], w_exp_down[0],
                      g_final)
    return (_trunk(x_prompt, mem_prompt, w), _trunk(x_sample, mem_sample, w))
```

```python
import functools

import jax
import jax.numpy as jnp
import numpy as np
from jax import lax
from jax.experimental import pallas as pl
from jax.experimental.pallas import tpu as pltpu

F32 = jnp.float32
BF16 = jnp.bfloat16

D_MODEL = 1024
N_MEM = 256
GRID_W = 64
HEAD_DIM = 64
D_SGU = 512
D_NA = 512
CHUNK = 128
NA_KH = 8
NA_KW = 16
XA_HEADS = 4
XA_HEAD_DIM = 128
D_XA = 512
N_GROUPS = 4
EXPERTS_PER_GROUP = 8
N_EXPERTS = 32
D_EXPERT = 128
EPS = 1e-6

LANES = 128
NEG_BIG = -1e30
TOKEN_TILE = 1024
NA_ROW_TILE = 8
VMEM_LIMIT_BYTES = 56 * 1024 * 1024


def _rms(x, g):
    return x * lax.rsqrt(jnp.mean(x * x, axis=-1, keepdims=True) + EPS) * g


def _dot(a, b):
    return jnp.dot(a, b, preferred_element_type=F32)


def _dot_nt(a, b):
    return lax.dot_general(a, b, (((1,), (1,)), ((), ())), preferred_element_type=F32)


def _whole(shape):
    nd = len(shape)
    return pl.BlockSpec(shape, lambda *_: (0,) * nd)


def _params(*sem, flags=None):
    return pltpu.CompilerParams(dimension_semantics=sem, vmem_limit_bytes=VMEM_LIMIT_BYTES, flags=flags)


def _mem_kv_body(mem_ref, g_ref, w_ref, k_ref, v_ref):
    memn = _rms(mem_ref[0], g_ref[...]).astype(BF16)
    kv = _dot(memn, w_ref[...])
    k_ref[0] = kv[:, :D_XA].astype(BF16)
    v_ref[0] = kv[:, D_XA:].astype(BF16)


def _mem_kv(mem, g_mem, w_xkv):
    b = mem.shape[0]
    out = jax.ShapeDtypeStruct((b, N_MEM, D_XA), BF16)
    return pl.pallas_call(
        _mem_kv_body,
        grid=(b,),
        in_specs=[pl.BlockSpec((1, N_MEM, D_MODEL), lambda i: (i, 0, 0)),
                  _whole((1, D_MODEL)), _whole((D_MODEL, 2 * D_XA))],
        out_specs=[pl.BlockSpec((1, N_MEM, D_XA), lambda i: (i, 0, 0))] * 2,
        out_shape=[out, out],
        compiler_params=_params("parallel"),
        name="mem_kv",
    )(mem, g_mem, w_xkv)


def _in_proj_body(x_ref, gmix_ref, wa_ref, wqv_ref, wkt_ref, gv_ref, wcat_ref, bexp_ref, goa_ref,
                  q_ref, kt_ref, v_ref, ya_ref):
    tm = x_ref.shape[0]
    hn = _rms(x_ref[...], gmix_ref[...]).astype(BF16)
    a = jax.nn.gelu(_dot(hn, wa_ref[...]))
    u = a[:, :D_SGU]
    vn = _rms(a[:, D_SGU:], gv_ref[...]).astype(BF16)
    lo = lax.broadcasted_iota(jnp.int32, (CHUNK, LANES), 1) < HEAD_DIM
    zero = jnp.zeros((CHUNK, LANES), BF16)
    chunks = []
    for c in range(tm // CHUNK):
        pairs = []
        for p in range(D_SGU // LANES):
            vp = vn[c * CHUNK:(c + 1) * CHUNK, p * LANES:(p + 1) * LANES]
            rhs = jnp.concatenate([jnp.where(lo, vp, zero), jnp.where(lo, zero, vp)], axis=0)
            pairs.append(_dot(wcat_ref[p], rhs))
        chunks.append(jnp.concatenate(pairs, axis=1) + bexp_ref[...])
    ya = u * jnp.concatenate(chunks, axis=0)
    ya_ref[...] = _rms(ya, goa_ref[...]).astype(BF16)
    qv = _dot(hn, wqv_ref[...])
    q_ref[...] = (qv[:, 0:D_NA] * (HEAD_DIM ** -0.5)).astype(BF16)
    v_ref[...] = qv[:, D_NA:].astype(BF16)
    kt = _dot_nt(wkt_ref[...], hn).astype(BF16)
    for c in range(tm // LANES):
        kt_ref[c] = kt[:, c * LANES:(c + 1) * LANES]


def _in_proj(x2, g_mix, w_a, w_qv, w_kt, g_sgu_v, wcat, bexp, g_out_a):
    n = x2.shape[0]
    tm = TOKEN_TILE
    out = jax.ShapeDtypeStruct((n, D_NA), BF16)
    tok = lambda w: pl.BlockSpec((tm, w), lambda i: (i, 0))
    return pl.pallas_call(
        _in_proj_body,
        grid=(n // tm,),
        in_specs=[tok(D_MODEL), _whole((1, D_MODEL)), _whole(w_a.shape), _whole(w_qv.shape),
                  _whole(w_kt.shape), _whole((1, D_SGU)), _whole(wcat.shape), _whole(bexp.shape),
                  _whole((1, D_SGU))],
        out_specs=[tok(D_NA), pl.BlockSpec((tm // LANES, D_NA, LANES), lambda i: (i, 0, 0)),
                   tok(D_NA), tok(D_NA)],
        out_shape=[out, jax.ShapeDtypeStruct((n // LANES, D_NA, LANES), BF16), out, out],
        compiler_params=_params("parallel"),
        name="in_proj",
    )(x2, g_mix, w_a, w_qv, w_kt, g_sgu_v, wcat, bexp, g_out_a)


ROWS_PER_SLAB = LANES // GRID_W
SLABS_PER_WINDOW = NA_KH // ROWS_PER_SLAB


def _na_body(rows, q_ref, kt_ref, v_ref, bm_ref, o_ref, kk_ref, s_ref, p_ref):
    n_slabs = rows // ROWS_PER_SLAB
    n_tiles = rows // NA_ROW_TILE
    half = LANES // 2

    def shift(s, carry):
        a = kt_ref[s]
        kk_ref[0, s] = a
        kk_ref[1, s] = jnp.concatenate([a[:, half:], kt_ref[s + 1][:, :half]], axis=1)
        return carry
    lax.fori_loop(0, n_slabs - 1, shift, 0, unroll=8)
    last = kt_ref[n_slabs - 1]
    kk_ref[0, n_slabs - 1] = last
    kk_ref[1, n_slabs - 1] = last

    s_ref[1] = jnp.zeros(s_ref.shape[1:], F32)
    p_ref[0] = jnp.zeros(p_ref.shape[1:], BF16)

    lo = lax.broadcasted_iota(jnp.int32, (GRID_W, LANES), 1) < HEAD_DIM
    zero = jnp.zeros((GRID_W, LANES), BF16)

    def first_key_row(r):
        return jnp.clip(r - NA_KH // 2, 0, rows - NA_KH)

    def tok(r):
        return pl.multiple_of(r * GRID_W, GRID_W)

    def step(k, carry):
        slot_a = k % 2
        slot_b = (k + 1) % 2
        tile_a = jnp.minimum(k, n_tiles - 1)
        tile_b = jnp.clip(k - 1, 0, n_tiles - 1)
        tile_c = jnp.clip(k - 2, 0, n_tiles - 1)
        for i in range(NA_ROW_TILE):
            r = tile_c * NA_ROW_TILE + i
            vw = v_ref[0, pl.ds(tok(first_key_row(r)), NA_KH * GRID_W), :]
            o = _dot(p_ref[slot_a, i], vw)
            o_ref[0, pl.ds(tok(r), GRID_W), :] = jnp.where(lo, o[:GRID_W], o[GRID_W:]).astype(BF16)
        for i in range(NA_ROW_TILE):
            r = tile_b * NA_ROW_TILE + i
            s = s_ref[slot_b, i] + bm_ref[0, r - first_key_row(r)]
            e = jnp.exp(s - jnp.max(s, axis=-1, keepdims=True))
            p_ref[slot_b, i] = (e / jnp.sum(e, axis=-1, keepdims=True)).astype(BF16)
        for i in range(NA_ROW_TILE):
            r = tile_a * NA_ROW_TILE + i
            rs = first_key_row(r)
            qrow = q_ref[0, pl.ds(tok(r), GRID_W), :]
            q2 = jnp.concatenate([jnp.where(lo, qrow, zero), jnp.where(lo, zero, qrow)], axis=0)
            odd = rs % ROWS_PER_SLAB
            s0 = rs // ROWS_PER_SLAB
            kw = jnp.concatenate([kk_ref[odd, s0 + j] for j in range(SLABS_PER_WINDOW)], axis=1)
            s_ref[slot_a, i] = _dot(q2, kw)
        return carry

    lax.fori_loop(0, n_tiles + 2, step, 0)


def _na(q, kt, v, bm):
    b, t, _ = q.shape
    rows = t // GRID_W
    assert rows >= 2 * NA_KH and rows % NA_ROW_TILE == 0
    n_pairs = D_NA // LANES
    n_slabs = t // LANES
    seq = pl.BlockSpec((1, t, LANES), lambda bi, p: (bi, 0, p))
    stage = (2, NA_ROW_TILE, 2 * GRID_W, NA_KH * GRID_W)
    return pl.pallas_call(
        functools.partial(_na_body, rows),
        grid=(b, n_pairs),
        in_specs=[seq, pl.BlockSpec((n_slabs, LANES, LANES), lambda bi, p: (bi, p, 0)), seq,
                  pl.BlockSpec((1, NA_KH, 2 * GRID_W, NA_KH * GRID_W), lambda bi, p: (p, 0, 0, 0))],
        out_specs=seq,
        out_shape=jax.ShapeDtypeStruct((b, t, D_NA), BF16),
        scratch_shapes=[pltpu.VMEM((2, n_slabs, LANES, LANES), BF16), pltpu.VMEM(stage, F32),
                        pltpu.VMEM(stage, BF16)],
        compiler_params=_params("parallel", "parallel"),
        name="na",
    )(q, kt, v, bm)


def _na_bias_table(rpb):
    c = np.arange(GRID_W)
    wstart = np.clip(c - NA_KW // 2, 0, GRID_W - NA_KW)
    kc = c[None, :]
    inwin = (kc >= wstart[:, None]) & (kc < wstart[:, None] + NA_KW)
    rel = np.clip(kc - c[:, None], -(NA_KW - 1), NA_KW - 1) + NA_KW - 1
    tab = jnp.where(jnp.asarray(inwin)[None, None], rpb[:, :, rel], NEG_BIG)
    bmh = jnp.stack([tab[:, NA_KH - 1 - d:2 * NA_KH - 1 - d] for d in range(NA_KH)], axis=1)
    h = rpb.shape[0]
    bmh = bmh.transpose(0, 1, 3, 2, 4).reshape(h // 2, 2, NA_KH, GRID_W, NA_KH * GRID_W)
    return bmh.transpose(0, 2, 1, 3, 4).reshape(h // 2, NA_KH, 2 * GRID_W, NA_KH * GRID_W).astype(F32)


def _mix_xattn_body(x_ref, ya_ref, yb_ref, km_ref, vm_ref, gob_ref, wout_ref, gx_ref, wxq_ref,
                    wxo_ref, h2_ref):
    ybn = _rms(yb_ref[0].astype(F32), gob_ref[...]).astype(BF16)
    h1 = x_ref[0] + _dot(ya_ref[0], wout_ref[0:D_SGU, :]) + _dot(ybn, wout_ref[D_SGU:, :])
    hn = _rms(h1, gx_ref[...]).astype(BF16)
    q = _dot(hn, wxq_ref[...]).astype(BF16)
    heads = []
    for h in range(XA_HEADS):
        sl = slice(h * XA_HEAD_DIM, (h + 1) * XA_HEAD_DIM)
        s = _dot_nt(q[:, sl], km_ref[0, :, sl]) * (XA_HEAD_DIM ** -0.5)
        e = jnp.exp(s - jnp.max(s, axis=-1, keepdims=True))
        o = _dot(e.astype(BF16), vm_ref[0, :, sl]) / jnp.sum(e, axis=-1, keepdims=True)
        heads.append(o.astype(BF16))
    h2_ref[0] = h1 + _dot(jnp.concatenate(heads, axis=1), wxo_ref[...])


def _mix_xattn(x, ya, yb, km, vm, g_out_b, w_out, g_xattn, w_xq, w_xo):
    b, t, _ = x.shape
    tm = TOKEN_TILE
    tok = lambda w: pl.BlockSpec((1, tm, w), lambda bi, ti: (bi, ti, 0))
    mem = pl.BlockSpec((1, N_MEM, D_XA), lambda bi, ti: (bi, 0, 0))
    return pl.pallas_call(
        _mix_xattn_body,
        grid=(b, t // tm),
        in_specs=[tok(D_MODEL), tok(D_SGU), tok(D_NA), mem, mem, _whole((1, D_NA)),
                  _whole(w_out.shape), _whole((1, D_MODEL)), _whole(w_xq.shape), _whole(w_xo.shape)],
        out_specs=tok(D_MODEL),
        out_shape=jax.ShapeDtypeStruct((b, t, D_MODEL), F32),
        compiler_params=_params("parallel", "parallel"),
        name="mix_xattn",
    )(x, ya, yb, km, vm, g_out_b, w_out, g_xattn, w_xq, w_xo)


def _route(logits, grp):
    lane = lax.broadcasted_iota(jnp.int32, logits.shape, 1)
    lg = jnp.where(lane < N_GROUPS, logits, NEG_BIG)
    gmax = jnp.max(lg, axis=-1, keepdims=True)
    l_sel = jnp.sum(jnp.where(lane == grp, logits, 0.0), axis=-1, keepdims=True)
    g_top = jnp.exp(l_sel - gmax) / jnp.sum(jnp.exp(lg - gmax), axis=-1, keepdims=True)
    first = N_GROUPS + grp * EXPERTS_PER_GROUP
    le = jnp.where((lane >= first) & (lane < first + EXPERTS_PER_GROUP), logits, NEG_BIG)
    m1 = jnp.max(le, axis=-1, keepdims=True)
    i1 = jnp.min(jnp.where(le == m1, lane, LANES), axis=-1, keepdims=True)
    le2 = jnp.where(lane == i1, NEG_BIG, le)
    m2 = jnp.max(le2, axis=-1, keepdims=True)
    i2 = jnp.min(jnp.where(le2 == m2, lane, LANES), axis=-1, keepdims=True)
    p2 = jnp.exp(m2 - m1)
    w1 = g_top / (1.0 + p2)
    w2 = g_top * p2 / (1.0 + p2)
    return jnp.where(lane == i1, w1, 0.0) + jnp.where(lane == i2, w2, 0.0)


def _split_bf16(x):
    hi = x.astype(BF16)
    return hi, (x - hi.astype(F32)).astype(BF16)


GROUP_ROWS = 8


def _group_body(h_ref, gm_ref, wrt_ref, gid_ref, cnt_ref):
    hn_hi, hn_lo = _split_bf16(_rms(h_ref[...], gm_ref[...]))
    part = _dot_nt(wrt_ref[...], hn_hi) + _dot_nt(wrt_ref[...], hn_lo)
    logit_t = part[:LANES] + part[LANES:]
    row = lax.broadcasted_iota(jnp.int32, (GROUP_ROWS, logit_t.shape[1]), 0)
    lg = jnp.where(row < N_GROUPS, logit_t[:GROUP_ROWS], NEG_BIG)
    gmax = jnp.max(lg, axis=0, keepdims=True)
    gid = jnp.min(jnp.where(lg == gmax, row, GROUP_ROWS), axis=0, keepdims=True)
    gid_ref[0] = gid

    @pl.when(pl.program_id(0) == 0)
    def _():
        cnt_ref[...] = jnp.zeros_like(cnt_ref)
    cnt_ref[...] += jnp.sum((row == gid).astype(F32), axis=1, keepdims=True)


def _group_of_tokens(h2, g_moe, wrt):
    n = h2.shape[0]
    tm = TOKEN_TILE
    return pl.pallas_call(
        _group_body,
        grid=(n // tm,),
        in_specs=[pl.BlockSpec((tm, D_MODEL), lambda i: (i, 0)), _whole((1, D_MODEL)), _whole(wrt.shape)],
        out_specs=[pl.BlockSpec((1, 1, tm), lambda i: (i, 0, 0)), _whole((GROUP_ROWS, LANES))],
        out_shape=[jax.ShapeDtypeStruct((n // tm, 1, tm), jnp.int32),
                   jax.ShapeDtypeStruct((GROUP_ROWS, LANES), F32)],
        compiler_params=_params("arbitrary"),
        name="moe_group",
    )(h2, g_moe, wrt)


def _slot_body(start_ref, gid_ref, before_ref, pos_ref, run_ref):
    @pl.when(pl.program_id(0) == 0)
    def _():
        run_ref[...] = jnp.zeros_like(run_ref)
    gid = gid_ref[0]
    row = lax.broadcasted_iota(jnp.int32, (GROUP_ROWS, gid.shape[1]), 0)
    onehot = row == gid
    earlier = _dot(onehot.astype(BF16), before_ref[...])
    row1 = lax.broadcasted_iota(jnp.int32, (GROUP_ROWS, 1), 0)
    start = jnp.zeros((GROUP_ROWS, 1), F32)
    for g in range(N_GROUPS):
        start = jnp.where(row1 == g, start_ref[g].astype(F32), start)
    slot = jnp.sum(jnp.where(onehot, earlier + (start + run_ref[:, 0:1]), 0.0), axis=0, keepdims=True)
    pos_ref[0] = slot.astype(jnp.int32)
    run_ref[...] += jnp.sum(onehot.astype(F32), axis=1, keepdims=True)


def _sorted_slots(start, gid, before):
    n_tiles, _, tm = gid.shape
    blk = pl.BlockSpec((1, 1, tm), lambda i, *_: (i, 0, 0))
    return pl.pallas_call(
        _slot_body,
        grid_spec=pltpu.PrefetchScalarGridSpec(
            num_scalar_prefetch=1, grid=(n_tiles,),
            in_specs=[blk, pl.BlockSpec(before.shape, lambda i, *_: (0, 0))], out_specs=blk,
            scratch_shapes=[pltpu.VMEM((GROUP_ROWS, LANES), F32)]),
        out_shape=jax.ShapeDtypeStruct(gid.shape, jnp.int32),
        compiler_params=_params("arbitrary"),
        name="moe_slots",
    )(start, gid, before)


N_FILL = N_GROUPS + 1
ROW_DMA_UNROLL = 8
SLAB = 8
assert SLAB * LANES == D_MODEL


def _rows_to_slabs(x, slab_ref):
    tm = x.shape[0]
    for k in range(SLAB):
        slab_ref[pl.ds(k, tm, stride=SLAB), :] = x[:, k * LANES:(k + 1) * LANES]


def _slabs_to_rows(slab_ref, tm):
    return jnp.concatenate([slab_ref[pl.ds(k, tm, stride=SLAB), :] for k in range(SLAB)], axis=1)


def _tile_of(r):
    return pl.ds(pl.multiple_of(r * SLAB, SLAB), SLAB)


def _start_row_copies(tm, copy_of):
    def chunk(c, carry):
        for u in range(ROW_DMA_UNROLL):
            copy_of(c * ROW_DMA_UNROLL + u).start(priority=u % 2)
        return carry
    lax.fori_loop(0, tm // ROW_DMA_UNROLL, chunk, 0)


def _scatter_body(fill_ref, pos_ref, h_ref, hs_ref, slab_ref, zero_ref, sem, fill_sem):
    tm = h_ref.shape[0]
    i = pl.program_id(0)
    slot = i % 2

    def wait_slot(sl):
        pltpu.make_async_copy(slab_ref.at[sl], hs_ref.at[pl.ds(0, tm * SLAB)], sem.at[sl]).wait()

    @pl.when(i >= 2)
    def _():
        wait_slot(slot)

    _rows_to_slabs(h_ref[...], slab_ref.at[slot])
    _start_row_copies(tm, lambda t: pltpu.make_async_copy(
        slab_ref.at[slot, _tile_of(t)], hs_ref.at[_tile_of(pos_ref[0, 0, t])], sem.at[slot]))

    @pl.when(i == 0)
    def _():
        zero_ref[...] = jnp.zeros_like(zero_ref)
        for f in range(N_FILL):
            def fill_copy(j, f=f):
                return pltpu.make_async_copy(zero_ref, hs_ref.at[_tile_of(fill_ref[f] + j)], fill_sem)

            def fill_start(j, carry, fill_copy=fill_copy):
                fill_copy(j).start()
                return carry

            def fill_wait(j, carry, fill_copy=fill_copy):
                fill_copy(j).wait()
                return carry
            lax.fori_loop(0, fill_ref[N_FILL + f], fill_start, 0)
            lax.fori_loop(0, fill_ref[N_FILL + f], fill_wait, 0)

    @pl.when(i == pl.num_programs(0) - 1)
    def _():
        @pl.when(i >= 1)
        def _():
            wait_slot(1 - slot)
        wait_slot(slot)


def _scatter_rows(fill, pos, h2, n_out):
    n_tiles, _, tm = pos.shape
    return pl.pallas_call(
        _scatter_body,
        grid_spec=pltpu.PrefetchScalarGridSpec(
            num_scalar_prefetch=1, grid=(n_tiles,),
            in_specs=[pl.BlockSpec((1, 1, tm), lambda i, *_: (i, 0, 0), memory_space=pltpu.SMEM),
                      pl.BlockSpec((tm, D_MODEL), lambda i, *_: (i, 0))],
            out_specs=pl.BlockSpec(memory_space=pl.ANY),
            scratch_shapes=[pltpu.VMEM((2, tm * SLAB, LANES), F32), pltpu.VMEM((SLAB, LANES), F32),
                            pltpu.SemaphoreType.DMA((2,)), pltpu.SemaphoreType.DMA]),
        out_shape=jax.ShapeDtypeStruct((n_out * SLAB, LANES), F32),
        compiler_params=_params("arbitrary"),
        name="moe_scatter_rows",
    )(fill, pos, h2)


def _gather_body(pos_ref, next_pos_ref, ys_ref, y_ref, slab_ref, sem):
    tm = y_ref.shape[0]
    i = pl.program_id(0)
    slot = i % 2

    def fetch(p_ref, sl):
        _start_row_copies(tm, lambda t: pltpu.make_async_copy(
            ys_ref.at[_tile_of(p_ref[0, 0, t])], slab_ref.at[sl, _tile_of(t)], sem.at[sl]))

    @pl.when(i == 0)
    def _():
        fetch(pos_ref, 0)

    @pl.when(i + 1 < pl.num_programs(0))
    def _():
        fetch(next_pos_ref, 1 - slot)

    pltpu.make_async_copy(ys_ref.at[pl.ds(0, tm * SLAB)], slab_ref.at[slot], sem.at[slot]).wait()
    y_ref[...] = _slabs_to_rows(slab_ref.at[slot], tm)


def _gather_rows(pos, ys, n):
    n_tiles, _, tm = pos.shape
    smem = lambda index_map: pl.BlockSpec((1, 1, tm), index_map, memory_space=pltpu.SMEM)
    return pl.pallas_call(
        _gather_body,
        grid=(n_tiles,),
        in_specs=[smem(lambda i: (i, 0, 0)), smem(lambda i: (jnp.minimum(i + 1, n_tiles - 1), 0, 0)),
                  pl.BlockSpec(memory_space=pl.ANY)],
        out_specs=pl.BlockSpec((tm, D_MODEL), lambda i: (i, 0)),
        out_shape=jax.ShapeDtypeStruct((n, D_MODEL), F32),
        scratch_shapes=[pltpu.VMEM((2, tm * SLAB, LANES), F32), pltpu.SemaphoreType.DMA((2,))],
        compiler_params=_params("arbitrary"),
        name="moe_gather_rows",
    )(pos, pos, ys)


def _moe_body(tile_ref, h_ref, gm_ref, wr_ref, wg_ref, wu_ref, wd_ref, gf_ref, y_ref):
    tm = h_ref.shape[0] // SLAB
    n_used = tile_ref[0]
    grp = tile_ref[1 + jnp.minimum(pl.program_id(0), n_used - 1)]
    h = _slabs_to_rows(h_ref, tm)
    hn_hi, hn_lo = _split_bf16(_rms(h, gm_ref[...]))
    part = _dot(hn_hi, wr_ref[...]) + _dot(hn_lo, wr_ref[...])
    gates = _route(part[:, :LANES] + part[:, LANES:], grp)
    lane = lax.broadcasted_iota(jnp.int32, gates.shape, 1)
    first = lax.broadcasted_iota(jnp.int32, (tm, 2 * D_EXPERT), 1) < D_EXPERT
    acts = []
    for j in range(EXPERTS_PER_GROUP // 2):
        e0 = N_GROUPS + grp * EXPERTS_PER_GROUP + 2 * j
        g0 = jnp.sum(jnp.where(lane == e0, gates, 0.0), axis=-1, keepdims=True)
        g1 = jnp.sum(jnp.where(lane == e0 + 1, gates, 0.0), axis=-1, keepdims=True)
        hg = _dot(hn_hi, wg_ref[j])
        hu = _dot(hn_hi, wu_ref[j])
        acts.append((jax.nn.silu(hg) * hu * jnp.where(first, g0, g1)).astype(BF16))
    y = _rms(_slabs_to_rows(h_ref, tm) + _dot(jnp.concatenate(acts, axis=1), wd_ref[0]), gf_ref[...])
    _rows_to_slabs(y, y_ref)


def _moe_sorted(tiles, hs, g_moe, wr, wg2, wu2, wd2, g_final):
    blk = TOKEN_TILE * SLAB
    n_steps = hs.shape[0] // blk
    ppg = EXPERTS_PER_GROUP // 2
    used = lambda s, tl: jnp.minimum(s, tl[0] - 1)
    grp = lambda s, tl: tl[1 + used(s, tl)]
    const = lambda shape: pl.BlockSpec(shape, lambda s, tl: (0,) * len(shape))
    return pl.pallas_call(
        _moe_body,
        grid_spec=pltpu.PrefetchScalarGridSpec(
            num_scalar_prefetch=1, grid=(n_steps,),
            in_specs=[pl.BlockSpec((blk, LANES), lambda s, tl: (used(s, tl), 0)), const((1, D_MODEL)),
                      const(wr.shape),
                      pl.BlockSpec((ppg, D_MODEL, 2 * D_EXPERT), lambda s, tl: (grp(s, tl), 0, 0)),
                      pl.BlockSpec((ppg, D_MODEL, 2 * D_EXPERT), lambda s, tl: (grp(s, tl), 0, 0)),
                      pl.BlockSpec((1, ppg * 2 * D_EXPERT, D_MODEL), lambda s, tl: (grp(s, tl), 0, 0)),
                      const((1, D_MODEL))],
            out_specs=pl.BlockSpec((blk, LANES), lambda s, tl: (s, 0))),
        out_shape=jax.ShapeDtypeStruct(hs.shape, F32),
        compiler_params=_params("arbitrary"),
        name="moe",
    )(tiles, hs, g_moe, wr, wg2, wu2, wd2, g_final)


def _moe(h2, w):
    n = h2.shape[0]
    tm = TOKEN_TILE
    n_steps = n // tm + N_GROUPS - 1
    gid, cnt = _group_of_tokens(h2, w["g_moe"], w["wrt"])
    cnt = cnt[:N_GROUPS, 0].astype(jnp.int32)
    padded = (cnt + tm - 1) // tm * tm
    end = jnp.cumsum(padded)
    start = end - padded
    tile_group = jnp.sum(jnp.arange(n_steps)[:, None] >= (end // tm)[None, :], axis=1)
    tiles = jnp.concatenate([end[-1:] // tm, jnp.minimum(tile_group, N_GROUPS - 1)]).astype(jnp.int32)
    fill = jnp.concatenate([start + cnt, end[-1:], padded - cnt, n_steps * tm - end[-1:]]).astype(jnp.int32)
    pos = _sorted_slots(start.astype(jnp.int32), gid, w["before"])
    hs = _scatter_rows(fill, pos, h2, n_steps * tm)
    ys = _moe_sorted(tiles, hs, w["g_moe"], w["wr"], w["wg2"], w["wu2"], w["wd2"], w["g_final"])
    return _gather_rows(pos, ys, n)


def _prep_weights(g_mix, w_in, g_sgu_v, sgu_w, sgu_b, na_rpb, g_out_a, g_out_b, w_out, g_xattn, g_mem,
                  w_xq, w_xkv, w_xo, g_moe, w_router_group, w_router_expert, w_exp_gate, w_exp_up,
                  w_exp_down, g_final):
    row = lambda g: g.reshape(1, -1).astype(F32)
    heads = sgu_w.shape[0]
    wcat = jnp.concatenate([sgu_w[0::2], sgu_w[1::2]], axis=2).astype(BF16)
    bexp = jnp.repeat(sgu_b.T, HEAD_DIM, axis=1).astype(F32)
    assert heads * HEAD_DIM == D_SGU
    wr = jnp.zeros((D_MODEL, LANES), F32)
    wr = wr.at[:, :N_GROUPS].set(w_router_group).at[:, N_GROUPS:N_GROUPS + N_EXPERTS].set(w_router_expert)
    wr_hi = wr.astype(BF16)
    wr_split = jnp.concatenate([wr_hi, (wr - wr_hi.astype(F32)).astype(BF16)], axis=1)
    before = jnp.triu(jnp.ones((TOKEN_TILE, TOKEN_TILE), BF16), k=1)
    pair_cols = lambda w: (w.reshape(N_EXPERTS // 2, 2, D_MODEL, D_EXPERT).transpose(0, 2, 1, 3)
                           .reshape(N_EXPERTS // 2, D_MODEL, 2 * D_EXPERT).astype(BF16))
    return dict(
        g_mix=row(g_mix), w_a=w_in[:, :2 * D_SGU].astype(BF16),
        w_qv=jnp.concatenate([w_in[:, 2 * D_SGU:2 * D_SGU + D_NA], w_in[:, 2 * D_SGU + 2 * D_NA:]], axis=1).astype(BF16),
        w_kt=w_in[:, 2 * D_SGU + D_NA:2 * D_SGU + 2 * D_NA].T.astype(BF16),
        g_sgu_v=row(g_sgu_v), wcat=wcat, bexp=bexp,
        bm=_na_bias_table(na_rpb), g_out_a=row(g_out_a), g_out_b=row(g_out_b), w_out=w_out.astype(BF16),
        g_xattn=row(g_xattn), g_mem=row(g_mem), w_xq=w_xq.astype(BF16), w_xkv=w_xkv.astype(BF16),
        w_xo=w_xo.astype(BF16), g_moe=row(g_moe), wr=wr_split, wrt=wr_split.T, before=before,
        wg2=pair_cols(w_exp_gate), wu2=pair_cols(w_exp_up),
        wd2=w_exp_down.reshape(N_GROUPS, EXPERTS_PER_GROUP * D_EXPERT, D_MODEL).astype(BF16),
        g_final=row(g_final))


def _trunk(x, mem, w):
    b, t, _ = x.shape
    km, vm = _mem_kv(mem, w["g_mem"], w["w_xkv"])
    q, kt, v, ya = _in_proj(x.reshape(b * t, D_MODEL), w["g_mix"], w["w_a"], w["w_qv"], w["w_kt"],
                            w["g_sgu_v"], w["wcat"], w["bexp"], w["g_out_a"])
    seq = lambda a: a.reshape(b, t, -1)
    yb = _na(seq(q), kt, seq(v), w["bm"])
    h2 = _mix_xattn(x, seq(ya), yb, km, vm, w["g_out_b"], w["w_out"], w["g_xattn"], w["w_xq"], w["w_xo"])
    y = _moe(h2.reshape(b * t, D_MODEL), w)
    return y.reshape(b, t, D_MODEL)


def kernel(x_prompt, x_sample, mem_prompt, mem_sample, g_mix, w_in, g_sgu_v, sgu_w, sgu_b, na_rpb, g_out_a,
           g_out_b, w_out, g_xattn, g_mem, w_xq, w_xkv, w_xo, g_moe, w_router_group, w_router_expert,
           w_exp_gate, w_exp_up, w_exp_down, g_final):
    assert g_mix.shape[0] == 1
    w = _prep_weights(g_mix[0], w_in[0], g_sgu_v[0], sgu_w[0], sgu_b[0], na_rpb[0], g_out_a[0], g_out_b[0],
                      w_out[0], g_xattn[0], g_mem[0], w_xq[0], w_xkv[0], w_xo[0], g_moe[0],
                      w_router_group[0], w_router_expert[0], w_exp_gate[0], w_exp_up[0], w_exp_down[0],
                      g_final)
    return (_trunk(x_prompt, mem_prompt, w), _trunk(x_sample, mem_sample, w))
```

```python
import functools
import math

import jax
import jax.numpy as jnp
import numpy as np
from jax import lax
from jax.experimental import pallas as pl
from jax.experimental.pallas import tpu as pltpu

F32 = jnp.float32
BF16 = jnp.bfloat16

D_MODEL = 1024
N_MEM = 256
GRID_W = 64
HEAD_DIM = 64
D_SGU = 512
D_NA = 512
CHUNK = 128
NA_KH = 8
NA_KW = 16
XA_HEADS = 4
XA_HEAD_DIM = 128
D_XA = 512
N_GROUPS = 4
EXPERTS_PER_GROUP = 8
N_EXPERTS = 32
D_EXPERT = 128
EPS = 1e-6

LANES = 128
NEG_BIG = -1e30
LOG2_E = math.log2(math.e)
TOKEN_TILE = 1024
NA_ROW_TILE = 8
VMEM_LIMIT_BYTES = 56 * 1024 * 1024


def _rms(x, g):
    return x * lax.rsqrt(jnp.mean(x * x, axis=-1, keepdims=True) + EPS) * g


def _dot(a, b):
    return jnp.dot(a, b, preferred_element_type=F32)


def _dot_nt(a, b):
    return lax.dot_general(a, b, (((1,), (1,)), ((), ())), preferred_element_type=F32)


def _whole(shape):
    nd = len(shape)
    return pl.BlockSpec(shape, lambda *_: (0,) * nd)


def _params(*sem, flags=None):
    return pltpu.CompilerParams(dimension_semantics=sem, vmem_limit_bytes=VMEM_LIMIT_BYTES, flags=flags)


def _mem_kv_body(mem_ref, g_ref, w_ref, k_ref, v_ref):
    memn = _rms(mem_ref[0], g_ref[...]).astype(BF16)
    kv = _dot(memn, w_ref[...])
    k_ref[0] = kv[:, :D_XA].astype(BF16)
    v_ref[0] = kv[:, D_XA:].astype(BF16)


def _mem_kv(mem, g_mem, w_xkv):
    b = mem.shape[0]
    out = jax.ShapeDtypeStruct((b, N_MEM, D_XA), BF16)
    return pl.pallas_call(
        _mem_kv_body,
        grid=(b,),
        in_specs=[pl.BlockSpec((1, N_MEM, D_MODEL), lambda i: (i, 0, 0)),
                  _whole((1, D_MODEL)), _whole((D_MODEL, 2 * D_XA))],
        out_specs=[pl.BlockSpec((1, N_MEM, D_XA), lambda i: (i, 0, 0))] * 2,
        out_shape=[out, out],
        compiler_params=_params("parallel"),
        name="mem_kv",
    )(mem, g_mem, w_xkv)


def _in_proj_body(x_ref, gmix_ref, wa_ref, wqv_ref, wkt_ref, gv_ref, wcat_ref, bexp_ref, goa_ref,
                  q_ref, kt_ref, v_ref, ya_ref):
    tm = x_ref.shape[0]
    hn = _rms(x_ref[...], gmix_ref[...]).astype(BF16)
    a = jax.nn.gelu(_dot(hn, wa_ref[...]))
    u = a[:, :D_SGU]
    vn = _rms(a[:, D_SGU:], gv_ref[...]).astype(BF16)
    lo = lax.broadcasted_iota(jnp.int32, (CHUNK, LANES), 1) < HEAD_DIM
    zero = jnp.zeros((CHUNK, LANES), BF16)
    chunks = []
    for c in range(tm // CHUNK):
        pairs = []
        for p in range(D_SGU // LANES):
            vp = vn[c * CHUNK:(c + 1) * CHUNK, p * LANES:(p + 1) * LANES]
            rhs = jnp.concatenate([jnp.where(lo, vp, zero), jnp.where(lo, zero, vp)], axis=0)
            pairs.append(_dot(wcat_ref[p], rhs))
        chunks.append(jnp.concatenate(pairs, axis=1) + bexp_ref[...])
    ya = u * jnp.concatenate(chunks, axis=0)
    ya_ref[...] = _rms(ya, goa_ref[...]).astype(BF16)
    qv = _dot(hn, wqv_ref[...])
    q_ref[...] = (qv[:, 0:D_NA] * (LOG2_E * HEAD_DIM ** -0.5)).astype(BF16)
    v_ref[...] = qv[:, D_NA:].astype(BF16)
    kt = _dot_nt(wkt_ref[...], hn).astype(BF16)
    for c in range(tm // LANES):
        kt_ref[c] = kt[:, c * LANES:(c + 1) * LANES]


def _in_proj(x2, g_mix, w_a, w_qv, w_kt, g_sgu_v, wcat, bexp, g_out_a):
    n = x2.shape[0]
    tm = TOKEN_TILE
    out = jax.ShapeDtypeStruct((n, D_NA), BF16)
    tok = lambda w: pl.BlockSpec((tm, w), lambda i: (i, 0))
    return pl.pallas_call(
        _in_proj_body,
        grid=(n // tm,),
        in_specs=[tok(D_MODEL), _whole((1, D_MODEL)), _whole(w_a.shape), _whole(w_qv.shape),
                  _whole(w_kt.shape), _whole((1, D_SGU)), _whole(wcat.shape), _whole(bexp.shape),
                  _whole((1, D_SGU))],
        out_specs=[tok(D_NA), pl.BlockSpec((tm // LANES, D_NA, LANES), lambda i: (i, 0, 0)),
                   tok(D_NA), tok(D_NA)],
        out_shape=[out, jax.ShapeDtypeStruct((n // LANES, D_NA, LANES), BF16), out, out],
        compiler_params=_params("parallel"),
        name="in_proj",
    )(x2, g_mix, w_a, w_qv, w_kt, g_sgu_v, wcat, bexp, g_out_a)


ROWS_PER_SLAB = LANES // GRID_W
SLABS_PER_WINDOW = NA_KH // ROWS_PER_SLAB


def _na_body(rows, q_ref, kt_ref, v_ref, bm_ref, o_ref, kk_ref, s_ref, p_ref):
    n_slabs = rows // ROWS_PER_SLAB
    n_tiles = rows // NA_ROW_TILE
    half = LANES // 2

    def shift(s, carry):
        a = kt_ref[s]
        kk_ref[0, s] = a
        kk_ref[1, s] = jnp.concatenate([a[:, half:], kt_ref[s + 1][:, :half]], axis=1)
        return carry
    lax.fori_loop(0, n_slabs - 1, shift, 0, unroll=8)
    last = kt_ref[n_slabs - 1]
    kk_ref[0, n_slabs - 1] = last
    kk_ref[1, n_slabs - 1] = last

    s_ref[1] = jnp.zeros(s_ref.shape[1:], F32)
    p_ref[0] = jnp.zeros(p_ref.shape[1:], BF16)

    lo = lax.broadcasted_iota(jnp.int32, (GRID_W, LANES), 1) < HEAD_DIM
    zero = jnp.zeros((GRID_W, LANES), BF16)

    def first_key_row(r):
        return jnp.clip(r - NA_KH // 2, 0, rows - NA_KH)

    def tok(r):
        return pl.multiple_of(r * GRID_W, GRID_W)

    def step(k, carry):
        slot_a = k % 2
        slot_b = (k + 1) % 2
        tile_a = jnp.minimum(k, n_tiles - 1)
        tile_b = jnp.clip(k - 1, 0, n_tiles - 1)
        tile_c = jnp.clip(k - 2, 0, n_tiles - 1)
        for i in range(NA_ROW_TILE):
            r = tile_c * NA_ROW_TILE + i
            vw = v_ref[0, pl.ds(tok(first_key_row(r)), NA_KH * GRID_W), :]
            o = _dot(p_ref[slot_a, i], vw)
            o_ref[0, pl.ds(tok(r), GRID_W), :] = jnp.where(lo, o[:GRID_W], o[GRID_W:]).astype(BF16)
        for i in range(NA_ROW_TILE):
            r = tile_b * NA_ROW_TILE + i
            s = s_ref[slot_b, i] + bm_ref[0, r - first_key_row(r)]
            e = jnp.exp2(s - jnp.max(s, axis=-1, keepdims=True))
            p_ref[slot_b, i] = (e / jnp.sum(e, axis=-1, keepdims=True)).astype(BF16)
        for i in range(NA_ROW_TILE):
            r = tile_a * NA_ROW_TILE + i
            rs = first_key_row(r)
            qrow = q_ref[0, pl.ds(tok(r), GRID_W), :]
            q2 = jnp.concatenate([jnp.where(lo, qrow, zero), jnp.where(lo, zero, qrow)], axis=0)
            odd = rs % ROWS_PER_SLAB
            s0 = rs // ROWS_PER_SLAB
            kw = jnp.concatenate([kk_ref[odd, s0 + j] for j in range(SLABS_PER_WINDOW)], axis=1)
            s_ref[slot_a, i] = _dot(q2, kw)
        return carry

    lax.fori_loop(0, n_tiles + 2, step, 0)


def _na(q, kt, v, bm):
    b, t, _ = q.shape
    rows = t // GRID_W
    assert rows >= 2 * NA_KH and rows % NA_ROW_TILE == 0
    n_pairs = D_NA // LANES
    n_slabs = t // LANES
    seq = pl.BlockSpec((1, t, LANES), lambda bi, p: (bi, 0, p))
    stage = (2, NA_ROW_TILE, 2 * GRID_W, NA_KH * GRID_W)
    return pl.pallas_call(
        functools.partial(_na_body, rows),
        grid=(b, n_pairs),
        in_specs=[seq, pl.BlockSpec((n_slabs, LANES, LANES), lambda bi, p: (bi, p, 0)), seq,
                  pl.BlockSpec((1, NA_KH, 2 * GRID_W, NA_KH * GRID_W), lambda bi, p: (p, 0, 0, 0))],
        out_specs=seq,
        out_shape=jax.ShapeDtypeStruct((b, t, D_NA), BF16),
        scratch_shapes=[pltpu.VMEM((2, n_slabs, LANES, LANES), BF16), pltpu.VMEM(stage, F32),
                        pltpu.VMEM(stage, BF16)],
        compiler_params=_params("parallel", "parallel"),
        name="na",
    )(q, kt, v, bm)


def _na_bias_table(rpb):
    c = np.arange(GRID_W)
    wstart = np.clip(c - NA_KW // 2, 0, GRID_W - NA_KW)
    kc = c[None, :]
    inwin = (kc >= wstart[:, None]) & (kc < wstart[:, None] + NA_KW)
    rel = np.clip(kc - c[:, None], -(NA_KW - 1), NA_KW - 1) + NA_KW - 1
    pick = jnp.asarray((rel[None] == np.arange(2 * NA_KW - 1)[:, None, None]).astype(np.float32))
    tab = jnp.einsum("hrk,kcn->hrcn", rpb * LOG2_E, pick, precision=lax.Precision.HIGHEST)
    tab = jnp.where(jnp.asarray(inwin)[None, None], tab, NEG_BIG)
    bmh = jnp.stack([tab[:, NA_KH - 1 - d:2 * NA_KH - 1 - d] for d in range(NA_KH)], axis=1)
    h = rpb.shape[0]
    bmh = bmh.transpose(0, 1, 3, 2, 4).reshape(h // 2, 2, NA_KH, GRID_W, NA_KH * GRID_W)
    return bmh.transpose(0, 2, 1, 3, 4).reshape(h // 2, NA_KH, 2 * GRID_W, NA_KH * GRID_W).astype(F32)


def _mix_xattn_body(x_ref, ya_ref, yb_ref, km_ref, vm_ref, gob_ref, wout_ref, gx_ref, wxq_ref,
                    wxo_ref, h2_ref):
    ybn = _rms(yb_ref[0].astype(F32), gob_ref[...]).astype(BF16)
    h1 = x_ref[0] + _dot(ya_ref[0], wout_ref[0:D_SGU, :]) + _dot(ybn, wout_ref[D_SGU:, :])
    hn = _rms(h1, gx_ref[...]).astype(BF16)
    q = _dot(hn, wxq_ref[...]).astype(BF16)
    heads = []
    for h in range(XA_HEADS):
        sl = slice(h * XA_HEAD_DIM, (h + 1) * XA_HEAD_DIM)
        s = _dot_nt(q[:, sl], km_ref[0, :, sl]) * (XA_HEAD_DIM ** -0.5)
        e = jnp.exp(s - jnp.max(s, axis=-1, keepdims=True))
        o = _dot(e.astype(BF16), vm_ref[0, :, sl]) / jnp.sum(e, axis=-1, keepdims=True)
        heads.append(o.astype(BF16))
    h2_ref[0] = h1 + _dot(jnp.concatenate(heads, axis=1), wxo_ref[...])


def _mix_xattn(x, ya, yb, km, vm, g_out_b, w_out, g_xattn, w_xq, w_xo):
    b, t, _ = x.shape
    tm = TOKEN_TILE
    tok = lambda w: pl.BlockSpec((1, tm, w), lambda bi, ti: (bi, ti, 0))
    mem = pl.BlockSpec((1, N_MEM, D_XA), lambda bi, ti: (bi, 0, 0))
    return pl.pallas_call(
        _mix_xattn_body,
        grid=(b, t // tm),
        in_specs=[tok(D_MODEL), tok(D_SGU), tok(D_NA), mem, mem, _whole((1, D_NA)),
                  _whole(w_out.shape), _whole((1, D_MODEL)), _whole(w_xq.shape), _whole(w_xo.shape)],
        out_specs=tok(D_MODEL),
        out_shape=jax.ShapeDtypeStruct((b, t, D_MODEL), F32),
        compiler_params=_params("parallel", "parallel"),
        name="mix_xattn",
    )(x, ya, yb, km, vm, g_out_b, w_out, g_xattn, w_xq, w_xo)


def _route(logits, grp):
    lane = lax.broadcasted_iota(jnp.int32, logits.shape, 1)
    lg = jnp.where(lane < N_GROUPS, logits, NEG_BIG)
    gmax = jnp.max(lg, axis=-1, keepdims=True)
    l_sel = jnp.sum(jnp.where(lane == grp, logits, 0.0), axis=-1, keepdims=True)
    g_top = jnp.exp(l_sel - gmax) / jnp.sum(jnp.exp(lg - gmax), axis=-1, keepdims=True)
    first = N_GROUPS + grp * EXPERTS_PER_GROUP
    le = jnp.where((lane >= first) & (lane < first + EXPERTS_PER_GROUP), logits, NEG_BIG)
    m1 = jnp.max(le, axis=-1, keepdims=True)
    i1 = jnp.min(jnp.where(le == m1, lane, LANES), axis=-1, keepdims=True)
    le2 = jnp.where(lane == i1, NEG_BIG, le)
    m2 = jnp.max(le2, axis=-1, keepdims=True)
    i2 = jnp.min(jnp.where(le2 == m2, lane, LANES), axis=-1, keepdims=True)
    p2 = jnp.exp(m2 - m1)
    w1 = g_top / (1.0 + p2)
    w2 = g_top * p2 / (1.0 + p2)
    return jnp.where(lane == i1, w1, 0.0) + jnp.where(lane == i2, w2, 0.0)


def _split_bf16(x):
    hi = x.astype(BF16)
    return hi, (x - hi.astype(F32)).astype(BF16)


GROUP_ROWS = 8


def _group_body(h_ref, gm_ref, wrt_ref, gid_ref, cnt_ref):
    hn_hi, hn_lo = _split_bf16(_rms(h_ref[...], gm_ref[...]))
    part = _dot_nt(wrt_ref[...], hn_hi) + _dot_nt(wrt_ref[...], hn_lo)
    logit_t = part[:LANES] + part[LANES:]
    row = lax.broadcasted_iota(jnp.int32, (GROUP_ROWS, logit_t.shape[1]), 0)
    lg = jnp.where(row < N_GROUPS, logit_t[:GROUP_ROWS], NEG_BIG)
    gmax = jnp.max(lg, axis=0, keepdims=True)
    gid = jnp.min(jnp.where(lg == gmax, row, GROUP_ROWS), axis=0, keepdims=True)
    gid_ref[0] = gid

    @pl.when(pl.program_id(0) == 0)
    def _():
        cnt_ref[...] = jnp.zeros_like(cnt_ref)
    cnt_ref[...] += jnp.sum((row == gid).astype(F32), axis=1, keepdims=True)


def _group_of_tokens(h2, g_moe, wrt):
    n = h2.shape[0]
    tm = TOKEN_TILE
    return pl.pallas_call(
        _group_body,
        grid=(n // tm,),
        in_specs=[pl.BlockSpec((tm, D_MODEL), lambda i: (i, 0)), _whole((1, D_MODEL)), _whole(wrt.shape)],
        out_specs=[pl.BlockSpec((1, 1, tm), lambda i: (i, 0, 0)), _whole((GROUP_ROWS, LANES))],
        out_shape=[jax.ShapeDtypeStruct((n // tm, 1, tm), jnp.int32),
                   jax.ShapeDtypeStruct((GROUP_ROWS, LANES), F32)],
        compiler_params=_params("arbitrary"),
        name="moe_group",
    )(h2, g_moe, wrt)


def _slot_body(start_ref, gid_ref, before_ref, pos_ref, run_ref):
    @pl.when(pl.program_id(0) == 0)
    def _():
        run_ref[...] = jnp.zeros_like(run_ref)
    gid = gid_ref[0]
    row = lax.broadcasted_iota(jnp.int32, (GROUP_ROWS, gid.shape[1]), 0)
    onehot = row == gid
    earlier = _dot(onehot.astype(BF16), before_ref[...])
    row1 = lax.broadcasted_iota(jnp.int32, (GROUP_ROWS, 1), 0)
    start = jnp.zeros((GROUP_ROWS, 1), F32)
    for g in range(N_GROUPS):
        start = jnp.where(row1 == g, start_ref[g].astype(F32), start)
    slot = jnp.sum(jnp.where(onehot, earlier + (start + run_ref[:, 0:1]), 0.0), axis=0, keepdims=True)
    pos_ref[0] = slot.astype(jnp.int32)
    run_ref[...] += jnp.sum(onehot.astype(F32), axis=1, keepdims=True)


def _sorted_slots(start, gid, before):
    n_tiles, _, tm = gid.shape
    blk = pl.BlockSpec((1, 1, tm), lambda i, *_: (i, 0, 0))
    return pl.pallas_call(
        _slot_body,
        grid_spec=pltpu.PrefetchScalarGridSpec(
            num_scalar_prefetch=1, grid=(n_tiles,),
            in_specs=[blk, pl.BlockSpec(before.shape, lambda i, *_: (0, 0))], out_specs=blk,
            scratch_shapes=[pltpu.VMEM((GROUP_ROWS, LANES), F32)]),
        out_shape=jax.ShapeDtypeStruct(gid.shape, jnp.int32),
        compiler_params=_params("arbitrary"),
        name="moe_slots",
    )(start, gid, before)


N_FILL = N_GROUPS + 1
ROW_DMA_UNROLL = 8
SLAB = 8
assert SLAB * LANES == D_MODEL


def _rows_to_slabs(x, slab_ref):
    tm = x.shape[0]
    for k in range(SLAB):
        slab_ref[pl.ds(k, tm, stride=SLAB), :] = x[:, k * LANES:(k + 1) * LANES]


def _slabs_to_rows(slab_ref, tm):
    return jnp.concatenate([slab_ref[pl.ds(k, tm, stride=SLAB), :] for k in range(SLAB)], axis=1)


def _tile_of(r):
    return pl.ds(pl.multiple_of(r * SLAB, SLAB), SLAB)


def _start_row_copies(tm, copy_of):
    def chunk(c, carry):
        for u in range(ROW_DMA_UNROLL):
            copy_of(c * ROW_DMA_UNROLL + u).start(priority=u % 2)
        return carry
    lax.fori_loop(0, tm // ROW_DMA_UNROLL, chunk, 0)


def _scatter_body(fill_ref, pos_ref, h_ref, hs_ref, slab_ref, zero_ref, sem, fill_sem):
    tm = h_ref.shape[0]
    i = pl.program_id(0)
    slot = i % 2

    def wait_slot(sl):
        pltpu.make_async_copy(slab_ref.at[sl], hs_ref.at[pl.ds(0, tm * SLAB)], sem.at[sl]).wait()

    @pl.when(i >= 2)
    def _():
        wait_slot(slot)

    _rows_to_slabs(h_ref[...], slab_ref.at[slot])
    _start_row_copies(tm, lambda t: pltpu.make_async_copy(
        slab_ref.at[slot, _tile_of(t)], hs_ref.at[_tile_of(pos_ref[0, 0, t])], sem.at[slot]))

    @pl.when(i == 0)
    def _():
        zero_ref[...] = jnp.zeros_like(zero_ref)
        for f in range(N_FILL):
            def fill_copy(j, f=f):
                return pltpu.make_async_copy(zero_ref, hs_ref.at[_tile_of(fill_ref[f] + j)], fill_sem)

            def fill_start(j, carry, fill_copy=fill_copy):
                fill_copy(j).start()
                return carry

            def fill_wait(j, carry, fill_copy=fill_copy):
                fill_copy(j).wait()
                return carry
            lax.fori_loop(0, fill_ref[N_FILL + f], fill_start, 0)
            lax.fori_loop(0, fill_ref[N_FILL + f], fill_wait, 0)

    @pl.when(i == pl.num_programs(0) - 1)
    def _():
        @pl.when(i >= 1)
        def _():
            wait_slot(1 - slot)
        wait_slot(slot)


def _scatter_rows(fill, pos, h2, n_out):
    n_tiles, _, tm = pos.shape
    return pl.pallas_call(
        _scatter_body,
        grid_spec=pltpu.PrefetchScalarGridSpec(
            num_scalar_prefetch=1, grid=(n_tiles,),
            in_specs=[pl.BlockSpec((1, 1, tm), lambda i, *_: (i, 0, 0), memory_space=pltpu.SMEM),
                      pl.BlockSpec((tm, D_MODEL), lambda i, *_: (i, 0))],
            out_specs=pl.BlockSpec(memory_space=pl.ANY),
            scratch_shapes=[pltpu.VMEM((2, tm * SLAB, LANES), F32), pltpu.VMEM((SLAB, LANES), F32),
                            pltpu.SemaphoreType.DMA((2,)), pltpu.SemaphoreType.DMA]),
        out_shape=jax.ShapeDtypeStruct((n_out * SLAB, LANES), F32),
        compiler_params=_params("arbitrary"),
        name="moe_scatter_rows",
    )(fill, pos, h2)


def _gather_body(pos_ref, next_pos_ref, ys_ref, y_ref, slab_ref, sem):
    tm = y_ref.shape[0]
    i = pl.program_id(0)
    slot = i % 2

    def fetch(p_ref, sl):
        _start_row_copies(tm, lambda t: pltpu.make_async_copy(
            ys_ref.at[_tile_of(p_ref[0, 0, t])], slab_ref.at[sl, _tile_of(t)], sem.at[sl]))

    @pl.when(i == 0)
    def _():
        fetch(pos_ref, 0)

    @pl.when(i + 1 < pl.num_programs(0))
    def _():
        fetch(next_pos_ref, 1 - slot)

    pltpu.make_async_copy(ys_ref.at[pl.ds(0, tm * SLAB)], slab_ref.at[slot], sem.at[slot]).wait()
    y_ref[...] = _slabs_to_rows(slab_ref.at[slot], tm)


def _gather_rows(pos, ys, n):
    n_tiles, _, tm = pos.shape
    smem = lambda index_map: pl.BlockSpec((1, 1, tm), index_map, memory_space=pltpu.SMEM)
    return pl.pallas_call(
        _gather_body,
        grid=(n_tiles,),
        in_specs=[smem(lambda i: (i, 0, 0)), smem(lambda i: (jnp.minimum(i + 1, n_tiles - 1), 0, 0)),
                  pl.BlockSpec(memory_space=pl.ANY)],
        out_specs=pl.BlockSpec((tm, D_MODEL), lambda i: (i, 0)),
        out_shape=jax.ShapeDtypeStruct((n, D_MODEL), F32),
        scratch_shapes=[pltpu.VMEM((2, tm * SLAB, LANES), F32), pltpu.SemaphoreType.DMA((2,))],
        compiler_params=_params("arbitrary"),
        name="moe_gather_rows",
    )(pos, pos, ys)


def _moe_body(tile_ref, h_ref, gm_ref, wr_ref, wg_ref, wu_ref, wd_ref, gf_ref, y_ref):
    tm = h_ref.shape[0] // SLAB
    n_used = tile_ref[0]
    grp = tile_ref[1 + jnp.minimum(pl.program_id(0), n_used - 1)]
    h = _slabs_to_rows(h_ref, tm)
    hn_hi, hn_lo = _split_bf16(_rms(h, gm_ref[...]))
    part = _dot(hn_hi, wr_ref[...]) + _dot(hn_lo, wr_ref[...])
    gates = _route(part[:, :LANES] + part[:, LANES:], grp)
    lane = lax.broadcasted_iota(jnp.int32, gates.shape, 1)
    first = lax.broadcasted_iota(jnp.int32, (tm, 2 * D_EXPERT), 1) < D_EXPERT
    acts = []
    for j in range(EXPERTS_PER_GROUP // 2):
        e0 = N_GROUPS + grp * EXPERTS_PER_GROUP + 2 * j
        g0 = jnp.sum(jnp.where(lane == e0, gates, 0.0), axis=-1, keepdims=True)
        g1 = jnp.sum(jnp.where(lane == e0 + 1, gates, 0.0), axis=-1, keepdims=True)
        hg = _dot(hn_hi, wg_ref[j])
        hu = _dot(hn_hi, wu_ref[j])
        acts.append((jax.nn.silu(hg) * hu * jnp.where(first, g0, g1)).astype(BF16))
    y = _rms(h + _dot(jnp.concatenate(acts, axis=1), wd_ref[0]), gf_ref[...])
    _rows_to_slabs(y, y_ref)


def _moe_sorted(tiles, hs, g_moe, wr, wg2, wu2, wd2, g_final):
    blk = TOKEN_TILE * SLAB
    n_steps = hs.shape[0] // blk
    ppg = EXPERTS_PER_GROUP // 2
    used = lambda s, tl: jnp.minimum(s, tl[0] - 1)
    grp = lambda s, tl: tl[1 + used(s, tl)]
    const = lambda shape: pl.BlockSpec(shape, lambda s, tl: (0,) * len(shape))
    return pl.pallas_call(
        _moe_body,
        grid_spec=pltpu.PrefetchScalarGridSpec(
            num_scalar_prefetch=1, grid=(n_steps,),
            in_specs=[pl.BlockSpec((blk, LANES), lambda s, tl: (used(s, tl), 0)), const((1, D_MODEL)),
                      const(wr.shape),
                      pl.BlockSpec((ppg, D_MODEL, 2 * D_EXPERT), lambda s, tl: (grp(s, tl), 0, 0)),
                      pl.BlockSpec((ppg, D_MODEL, 2 * D_EXPERT), lambda s, tl: (grp(s, tl), 0, 0)),
                      pl.BlockSpec((1, ppg * 2 * D_EXPERT, D_MODEL), lambda s, tl: (grp(s, tl), 0, 0)),
                      const((1, D_MODEL))],
            out_specs=pl.BlockSpec((blk, LANES), lambda s, tl: (s, 0))),
        out_shape=jax.ShapeDtypeStruct(hs.shape, F32),
        compiler_params=_params("arbitrary"),
        name="moe",
    )(tiles, hs, g_moe, wr, wg2, wu2, wd2, g_final)


def _moe(h2, w):
    n = h2.shape[0]
    tm = TOKEN_TILE
    n_steps = n // tm + N_GROUPS - 1
    gid, cnt = _group_of_tokens(h2, w["g_moe"], w["wrt"])
    cnt = cnt[:N_GROUPS, 0].astype(jnp.int32)
    padded = (cnt + tm - 1) // tm * tm
    end = jnp.cumsum(padded)
    start = end - padded
    tile_group = jnp.sum(jnp.arange(n_steps)[:, None] >= (end // tm)[None, :], axis=1)
    tiles = jnp.concatenate([end[-1:] // tm, jnp.minimum(tile_group, N_GROUPS - 1)]).astype(jnp.int32)
    fill = jnp.concatenate([start + cnt, end[-1:], padded - cnt, n_steps * tm - end[-1:]]).astype(jnp.int32)
    pos = _sorted_slots(start.astype(jnp.int32), gid, w["before"])
    hs = _scatter_rows(fill, pos, h2, n_steps * tm)
    ys = _moe_sorted(tiles, hs, w["g_moe"], w["wr"], w["wg2"], w["wu2"], w["wd2"], w["g_final"])
    return _gather_rows(pos, ys, n)


def _prep_weights(g_mix, w_in, g_sgu_v, sgu_w, sgu_b, na_rpb, g_out_a, g_out_b, w_out, g_xattn, g_mem,
                  w_xq, w_xkv, w_xo, g_moe, w_router_group, w_router_expert, w_exp_gate, w_exp_up,
                  w_exp_down, g_final):
    row = lambda g: g.reshape(1, -1).astype(F32)
    heads = sgu_w.shape[0]
    wcat = jnp.concatenate([sgu_w[0::2], sgu_w[1::2]], axis=2).astype(BF16)
    bexp = jnp.repeat(sgu_b.T, HEAD_DIM, axis=1).astype(F32)
    assert heads * HEAD_DIM == D_SGU
    wr = jnp.zeros((D_MODEL, LANES), F32)
    wr = wr.at[:, :N_GROUPS].set(w_router_group).at[:, N_GROUPS:N_GROUPS + N_EXPERTS].set(w_router_expert)
    wr_hi = wr.astype(BF16)
    wr_split = jnp.concatenate([wr_hi, (wr - wr_hi.astype(F32)).astype(BF16)], axis=1)
    before = jnp.triu(jnp.ones((TOKEN_TILE, TOKEN_TILE), BF16), k=1)
    pair_cols = lambda w: (w.reshape(N_EXPERTS // 2, 2, D_MODEL, D_EXPERT).transpose(0, 2, 1, 3)
                           .reshape(N_EXPERTS // 2, D_MODEL, 2 * D_EXPERT).astype(BF16))
    return dict(
        g_mix=row(g_mix), w_a=w_in[:, :2 * D_SGU].astype(BF16),
        w_qv=jnp.concatenate([w_in[:, 2 * D_SGU:2 * D_SGU + D_NA], w_in[:, 2 * D_SGU + 2 * D_NA:]], axis=1).astype(BF16),
        w_kt=w_in[:, 2 * D_SGU + D_NA:2 * D_SGU + 2 * D_NA].T.astype(BF16),
        g_sgu_v=row(g_sgu_v), wcat=wcat, bexp=bexp,
        bm=_na_bias_table(na_rpb), g_out_a=row(g_out_a), g_out_b=row(g_out_b), w_out=w_out.astype(BF16),
        g_xattn=row(g_xattn), g_mem=row(g_mem), w_xq=w_xq.astype(BF16), w_xkv=w_xkv.astype(BF16),
        w_xo=w_xo.astype(BF16), g_moe=row(g_moe), wr=wr_split, wrt=wr_split.T, before=before,
        wg2=pair_cols(w_exp_gate), wu2=pair_cols(w_exp_up),
        wd2=w_exp_down.reshape(N_GROUPS, EXPERTS_PER_GROUP * D_EXPERT, D_MODEL).astype(BF16),
        g_final=row(g_final))


def _trunk(x, mem, w):
    b, t, _ = x.shape
    km, vm = _mem_kv(mem, w["g_mem"], w["w_xkv"])
    q, kt, v, ya = _in_proj(x.reshape(b * t, D_MODEL), w["g_mix"], w["w_a"], w["w_qv"], w["w_kt"],
                            w["g_sgu_v"], w["wcat"], w["bexp"], w["g_out_a"])
    seq = lambda a: a.reshape(b, t, -1)
    yb = _na(seq(q), kt, seq(v), w["bm"])
    h2 = _mix_xattn(x, seq(ya), yb, km, vm, w["g_out_b"], w["w_out"], w["g_xattn"], w["w_xq"], w["w_xo"])
    y = _moe(h2.reshape(b * t, D_MODEL), w)
    return y.reshape(b, t, D_MODEL)


def kernel(x_prompt, x_sample, mem_prompt, mem_sample, g_mix, w_in, g_sgu_v, sgu_w, sgu_b, na_rpb, g_out_a,
           g_out_b, w_out, g_xattn, g_mem, w_xq, w_xkv, w_xo, g_moe, w_router_group, w_router_expert,
           w_exp_gate, w_exp_up, w_exp_down, g_final):
    assert g_mix.shape[0] == 1
    w = _prep_weights(g_mix[0], w_in[0], g_sgu_v[0], sgu_w[0], sgu_b[0], na_rpb[0], g_out_a[0], g_out_b[0],
                      w_out[0], g_xattn[0], g_mem[0], w_xq[0], w_xkv[0], w_xo[0], g_moe[0],
                      w_router_group[0], w_router_expert[0], w_exp_gate[0], w_exp_up[0], w_exp_down[0],
                      g_final)
    return (_trunk(x_prompt, mem_prompt, w), _trunk(x_sample, mem_sample, w))
```

```python
import functools
import math

import jax
import jax.numpy as jnp
import numpy as np
from jax import lax
from jax.experimental import pallas as pl
from jax.experimental.pallas import tpu as pltpu

F32 = jnp.float32
BF16 = jnp.bfloat16

D_MODEL = 1024
N_MEM = 256
GRID_W = 64
HEAD_DIM = 64
D_SGU = 512
D_NA = 512
CHUNK = 128
NA_KH = 8
NA_KW = 16
XA_HEADS = 4
XA_HEAD_DIM = 128
D_XA = 512
N_GROUPS = 4
EXPERTS_PER_GROUP = 8
N_EXPERTS = 32
D_EXPERT = 128
EPS = 1e-6

LANES = 128
NEG_BIG = -1e30
LOG2_E = math.log2(math.e)
TOKEN_TILE = 1024
NA_ROW_TILE = 8
VMEM_LIMIT_BYTES = 56 * 1024 * 1024


def _rms(x, g):
    return x * lax.rsqrt(jnp.mean(x * x, axis=-1, keepdims=True) + EPS) * g


def _dot(a, b):
    return jnp.dot(a, b, preferred_element_type=F32)


def _dot_nt(a, b):
    return lax.dot_general(a, b, (((1,), (1,)), ((), ())), preferred_element_type=F32)


def _whole(shape):
    nd = len(shape)
    return pl.BlockSpec(shape, lambda *_: (0,) * nd)


def _params(*sem, flags=None):
    return pltpu.CompilerParams(dimension_semantics=sem, vmem_limit_bytes=VMEM_LIMIT_BYTES, flags=flags)


def _mem_kv_body(mem_ref, g_ref, w_ref, k_ref, v_ref):
    memn = _rms(mem_ref[0], g_ref[...]).astype(BF16)
    kv = _dot(memn, w_ref[...])
    k_ref[0] = kv[:, :D_XA].astype(BF16)
    v_ref[0] = kv[:, D_XA:].astype(BF16)


def _mem_kv(mem, g_mem, w_xkv):
    b = mem.shape[0]
    out = jax.ShapeDtypeStruct((b, N_MEM, D_XA), BF16)
    return pl.pallas_call(
        _mem_kv_body,
        grid=(b,),
        in_specs=[pl.BlockSpec((1, N_MEM, D_MODEL), lambda i: (i, 0, 0)),
                  _whole((1, D_MODEL)), _whole((D_MODEL, 2 * D_XA))],
        out_specs=[pl.BlockSpec((1, N_MEM, D_XA), lambda i: (i, 0, 0))] * 2,
        out_shape=[out, out],
        compiler_params=_params("parallel"),
        name="mem_kv",
    )(mem, g_mem, w_xkv)


def _in_proj_body(x_ref, gmix_ref, wa_ref, wqv_ref, wkt_ref, gv_ref, wcat_ref, bexp_ref, goa_ref,
                  q_ref, kt_ref, v_ref, ya_ref):
    tm = x_ref.shape[0]
    hn = _rms(x_ref[...], gmix_ref[...]).astype(BF16)
    a = jax.nn.gelu(_dot(hn, wa_ref[...]))
    u = a[:, :D_SGU]
    vn = _rms(a[:, D_SGU:], gv_ref[...]).astype(BF16)
    lo = lax.broadcasted_iota(jnp.int32, (CHUNK, LANES), 1) < HEAD_DIM
    zero = jnp.zeros((CHUNK, LANES), BF16)
    chunks = []
    for c in range(tm // CHUNK):
        pairs = []
        for p in range(D_SGU // LANES):
            vp = vn[c * CHUNK:(c + 1) * CHUNK, p * LANES:(p + 1) * LANES]
            rhs = jnp.concatenate([jnp.where(lo, vp, zero), jnp.where(lo, zero, vp)], axis=0)
            pairs.append(_dot(wcat_ref[p], rhs))
        chunks.append(jnp.concatenate(pairs, axis=1) + bexp_ref[...])
    ya = u * jnp.concatenate(chunks, axis=0)
    ya_ref[...] = _rms(ya, goa_ref[...]).astype(BF16)
    qv = _dot(hn, wqv_ref[...])
    q_ref[...] = (qv[:, 0:D_NA] * (LOG2_E * HEAD_DIM ** -0.5)).astype(BF16)
    v_ref[...] = qv[:, D_NA:].astype(BF16)
    kt = _dot_nt(wkt_ref[...], hn).astype(BF16)
    for c in range(tm // LANES):
        kt_ref[c] = kt[:, c * LANES:(c + 1) * LANES]


def _in_proj(x2, g_mix, w_a, w_qv, w_kt, g_sgu_v, wcat, bexp, g_out_a):
    n = x2.shape[0]
    tm = TOKEN_TILE
    out = jax.ShapeDtypeStruct((n, D_NA), BF16)
    tok = lambda w: pl.BlockSpec((tm, w), lambda i: (i, 0))
    return pl.pallas_call(
        _in_proj_body,
        grid=(n // tm,),
        in_specs=[tok(D_MODEL), _whole((1, D_MODEL)), _whole(w_a.shape), _whole(w_qv.shape),
                  _whole(w_kt.shape), _whole((1, D_SGU)), _whole(wcat.shape), _whole(bexp.shape),
                  _whole((1, D_SGU))],
        out_specs=[tok(D_NA), pl.BlockSpec((tm // LANES, D_NA, LANES), lambda i: (i, 0, 0)),
                   tok(D_NA), tok(D_NA)],
        out_shape=[out, jax.ShapeDtypeStruct((n // LANES, D_NA, LANES), BF16), out, out],
        compiler_params=_params("parallel"),
        name="in_proj",
    )(x2, g_mix, w_a, w_qv, w_kt, g_sgu_v, wcat, bexp, g_out_a)


ROWS_PER_SLAB = LANES // GRID_W
SLABS_PER_WINDOW = NA_KH // ROWS_PER_SLAB


def _na_body(rows, q_ref, kt_ref, v_ref, bm_ref, o_ref, kk_ref, s_ref, p_ref):
    n_slabs = rows // ROWS_PER_SLAB
    n_tiles = rows // NA_ROW_TILE
    half = LANES // 2

    def shift(s, carry):
        a = kt_ref[s]
        kk_ref[0, s] = a
        kk_ref[1, s] = jnp.concatenate([a[:, half:], kt_ref[s + 1][:, :half]], axis=1)
        return carry
    lax.fori_loop(0, n_slabs - 1, shift, 0, unroll=8)
    last = kt_ref[n_slabs - 1]
    kk_ref[0, n_slabs - 1] = last
    kk_ref[1, n_slabs - 1] = last

    s_ref[1] = jnp.zeros(s_ref.shape[1:], F32)
    p_ref[0] = jnp.zeros(p_ref.shape[1:], BF16)

    lo = lax.broadcasted_iota(jnp.int32, (GRID_W, LANES), 1) < HEAD_DIM
    zero = jnp.zeros((GRID_W, LANES), BF16)

    def first_key_row(r):
        return jnp.clip(r - NA_KH // 2, 0, rows - NA_KH)

    def tok(r):
        return pl.multiple_of(r * GRID_W, GRID_W)

    def step(k, carry):
        slot_a = k % 2
        slot_b = (k + 1) % 2
        tile_a = jnp.minimum(k, n_tiles - 1)
        tile_b = jnp.clip(k - 1, 0, n_tiles - 1)
        tile_c = jnp.clip(k - 2, 0, n_tiles - 1)
        for i in range(NA_ROW_TILE):
            r = tile_c * NA_ROW_TILE + i
            vw = v_ref[0, pl.ds(tok(first_key_row(r)), NA_KH * GRID_W), :]
            o = _dot(p_ref[slot_a, i], vw)
            o_ref[0, pl.ds(tok(r), GRID_W), :] = jnp.where(lo, o[:GRID_W], o[GRID_W:]).astype(BF16)
        for i in range(NA_ROW_TILE):
            r = tile_b * NA_ROW_TILE + i
            s = s_ref[slot_b, i] + bm_ref[0, r - first_key_row(r)]
            e = jnp.exp2(s - jnp.max(s, axis=-1, keepdims=True))
            p_ref[slot_b, i] = (e / jnp.sum(e, axis=-1, keepdims=True)).astype(BF16)
        for i in range(NA_ROW_TILE):
            r = tile_a * NA_ROW_TILE + i
            rs = first_key_row(r)
            qrow = q_ref[0, pl.ds(tok(r), GRID_W), :]
            q2 = jnp.concatenate([jnp.where(lo, qrow, zero), jnp.where(lo, zero, qrow)], axis=0)
            odd = rs % ROWS_PER_SLAB
            s0 = rs // ROWS_PER_SLAB
            kw = jnp.concatenate([kk_ref[odd, s0 + j] for j in range(SLABS_PER_WINDOW)], axis=1)
            s_ref[slot_a, i] = _dot(q2, kw)
        return carry

    lax.fori_loop(0, n_tiles + 2, step, 0)


def _na(q, kt, v, bm):
    b, t, _ = q.shape
    rows = t // GRID_W
    assert rows >= 2 * NA_KH and rows % NA_ROW_TILE == 0
    n_pairs = D_NA // LANES
    n_slabs = t // LANES
    seq = pl.BlockSpec((1, t, LANES), lambda bi, p: (bi, 0, p))
    stage = (2, NA_ROW_TILE, 2 * GRID_W, NA_KH * GRID_W)
    return pl.pallas_call(
        functools.partial(_na_body, rows),
        grid=(b, n_pairs),
        in_specs=[seq, pl.BlockSpec((n_slabs, LANES, LANES), lambda bi, p: (bi, p, 0)), seq,
                  pl.BlockSpec((1, NA_KH, 2 * GRID_W, NA_KH * GRID_W), lambda bi, p: (p, 0, 0, 0))],
        out_specs=seq,
        out_shape=jax.ShapeDtypeStruct((b, t, D_NA), BF16),
        scratch_shapes=[pltpu.VMEM((2, n_slabs, LANES, LANES), BF16), pltpu.VMEM(stage, F32),
                        pltpu.VMEM(stage, BF16)],
        compiler_params=_params("parallel", "parallel"),
        name="na",
    )(q, kt, v, bm)


def _na_bias_table(rpb):
    c = np.arange(GRID_W)
    wstart = np.clip(c - NA_KW // 2, 0, GRID_W - NA_KW)
    kc = c[None, :]
    inwin = (kc >= wstart[:, None]) & (kc < wstart[:, None] + NA_KW)
    rel = np.clip(kc - c[:, None], -(NA_KW - 1), NA_KW - 1) + NA_KW - 1
    pick = jnp.asarray((rel[None] == np.arange(2 * NA_KW - 1)[:, None, None]).astype(np.float32))
    tab = jnp.einsum("hrk,kcn->hrcn", rpb * LOG2_E, pick, precision=lax.Precision.HIGHEST)
    tab = jnp.where(jnp.asarray(inwin)[None, None], tab, NEG_BIG)
    bmh = jnp.stack([tab[:, NA_KH - 1 - d:2 * NA_KH - 1 - d] for d in range(NA_KH)], axis=1)
    h = rpb.shape[0]
    bmh = bmh.transpose(0, 1, 3, 2, 4).reshape(h // 2, 2, NA_KH, GRID_W, NA_KH * GRID_W)
    return bmh.transpose(0, 2, 1, 3, 4).reshape(h // 2, NA_KH, 2 * GRID_W, NA_KH * GRID_W).astype(F32)


def _mix_xattn_body(x_ref, ya_ref, yb_ref, km_ref, vm_ref, gob_ref, wout_ref, gx_ref, wxq_ref,
                    wxo_ref, h2_ref):
    ybn = _rms(yb_ref[0].astype(F32), gob_ref[...]).astype(BF16)
    h1 = x_ref[0] + _dot(ya_ref[0], wout_ref[0:D_SGU, :]) + _dot(ybn, wout_ref[D_SGU:, :])
    hn = _rms(h1, gx_ref[...]).astype(BF16)
    q = _dot(hn, wxq_ref[...]).astype(BF16)
    heads = []
    for h in range(XA_HEADS):
        sl = slice(h * XA_HEAD_DIM, (h + 1) * XA_HEAD_DIM)
        s = _dot_nt(q[:, sl], km_ref[0, :, sl]) * (XA_HEAD_DIM ** -0.5)
        e = jnp.exp(s - jnp.max(s, axis=-1, keepdims=True))
        o = _dot(e.astype(BF16), vm_ref[0, :, sl]) / jnp.sum(e, axis=-1, keepdims=True)
        heads.append(o.astype(BF16))
    h2_ref[0] = h1 + _dot(jnp.concatenate(heads, axis=1), wxo_ref[...])


def _mix_xattn(x, ya, yb, km, vm, g_out_b, w_out, g_xattn, w_xq, w_xo):
    b, t, _ = x.shape
    tm = TOKEN_TILE
    tok = lambda w: pl.BlockSpec((1, tm, w), lambda bi, ti: (bi, ti, 0))
    mem = pl.BlockSpec((1, N_MEM, D_XA), lambda bi, ti: (bi, 0, 0))
    return pl.pallas_call(
        _mix_xattn_body,
        grid=(b, t // tm),
        in_specs=[tok(D_MODEL), tok(D_SGU), tok(D_NA), mem, mem, _whole((1, D_NA)),
                  _whole(w_out.shape), _whole((1, D_MODEL)), _whole(w_xq.shape), _whole(w_xo.shape)],
        out_specs=tok(D_MODEL),
        out_shape=jax.ShapeDtypeStruct((b, t, D_MODEL), F32),
        compiler_params=_params("parallel", "parallel"),
        name="mix_xattn",
    )(x, ya, yb, km, vm, g_out_b, w_out, g_xattn, w_xq, w_xo)


def _route(logits, grp):
    lane = lax.broadcasted_iota(jnp.int32, logits.shape, 1)
    lg = jnp.where(lane < N_GROUPS, logits, NEG_BIG)
    gmax = jnp.max(lg, axis=-1, keepdims=True)
    l_sel = jnp.sum(jnp.where(lane == grp, logits, 0.0), axis=-1, keepdims=True)
    g_top = jnp.exp(l_sel - gmax) / jnp.sum(jnp.exp(lg - gmax), axis=-1, keepdims=True)
    first = N_GROUPS + grp * EXPERTS_PER_GROUP
    le = jnp.where((lane >= first) & (lane < first + EXPERTS_PER_GROUP), logits, NEG_BIG)
    m1 = jnp.max(le, axis=-1, keepdims=True)
    i1 = jnp.min(jnp.where(le == m1, lane, LANES), axis=-1, keepdims=True)
    le2 = jnp.where(lane == i1, NEG_BIG, le)
    m2 = jnp.max(le2, axis=-1, keepdims=True)
    i2 = jnp.min(jnp.where(le2 == m2, lane, LANES), axis=-1, keepdims=True)
    p2 = jnp.exp(m2 - m1)
    w1 = g_top / (1.0 + p2)
    w2 = g_top * p2 / (1.0 + p2)
    return jnp.where(lane == i1, w1, 0.0) + jnp.where(lane == i2, w2, 0.0)


def _split_bf16(x):
    hi = x.astype(BF16)
    return hi, (x - hi.astype(F32)).astype(BF16)


GROUP_ROWS = 8


def _group_body(h_ref, gm_ref, wrt_ref, gid_ref, cnt_ref):
    hn_hi, hn_lo = _split_bf16(_rms(h_ref[...], gm_ref[...]))
    part = _dot_nt(wrt_ref[...], hn_hi) + _dot_nt(wrt_ref[...], hn_lo)
    logit_t = part[:LANES] + part[LANES:]
    row = lax.broadcasted_iota(jnp.int32, (GROUP_ROWS, logit_t.shape[1]), 0)
    lg = jnp.where(row < N_GROUPS, logit_t[:GROUP_ROWS], NEG_BIG)
    gmax = jnp.max(lg, axis=0, keepdims=True)
    gid = jnp.min(jnp.where(lg == gmax, row, GROUP_ROWS), axis=0, keepdims=True)
    gid_ref[0] = gid

    @pl.when(pl.program_id(0) == 0)
    def _():
        cnt_ref[...] = jnp.zeros_like(cnt_ref)
    cnt_ref[...] += jnp.sum((row == gid).astype(F32), axis=1, keepdims=True)


def _group_of_tokens(h2, g_moe, wrt):
    n = h2.shape[0]
    tm = TOKEN_TILE
    return pl.pallas_call(
        _group_body,
        grid=(n // tm,),
        in_specs=[pl.BlockSpec((tm, D_MODEL), lambda i: (i, 0)), _whole((1, D_MODEL)), _whole(wrt.shape)],
        out_specs=[pl.BlockSpec((1, 1, tm), lambda i: (i, 0, 0)), _whole((GROUP_ROWS, LANES))],
        out_shape=[jax.ShapeDtypeStruct((n // tm, 1, tm), jnp.int32),
                   jax.ShapeDtypeStruct((GROUP_ROWS, LANES), F32)],
        compiler_params=_params("arbitrary"),
        name="moe_group",
    )(h2, g_moe, wrt)


def _slot_body(start_ref, gid_ref, before_ref, pos_ref, run_ref):
    @pl.when(pl.program_id(0) == 0)
    def _():
        run_ref[...] = jnp.zeros_like(run_ref)
    gid = gid_ref[0]
    row = lax.broadcasted_iota(jnp.int32, (GROUP_ROWS, gid.shape[1]), 0)
    onehot = row == gid
    earlier = _dot(onehot.astype(BF16), before_ref[...])
    row1 = lax.broadcasted_iota(jnp.int32, (GROUP_ROWS, 1), 0)
    start = jnp.zeros((GROUP_ROWS, 1), F32)
    for g in range(N_GROUPS):
        start = jnp.where(row1 == g, start_ref[g].astype(F32), start)
    slot = jnp.sum(jnp.where(onehot, earlier + (start + run_ref[:, 0:1]), 0.0), axis=0, keepdims=True)
    pos_ref[0] = slot.astype(jnp.int32)
    run_ref[...] += jnp.sum(onehot.astype(F32), axis=1, keepdims=True)


def _sorted_slots(start, gid, before):
    n_tiles, _, tm = gid.shape
    blk = pl.BlockSpec((1, 1, tm), lambda i, *_: (i, 0, 0))
    return pl.pallas_call(
        _slot_body,
        grid_spec=pltpu.PrefetchScalarGridSpec(
            num_scalar_prefetch=1, grid=(n_tiles,),
            in_specs=[blk, pl.BlockSpec(before.shape, lambda i, *_: (0, 0))], out_specs=blk,
            scratch_shapes=[pltpu.VMEM((GROUP_ROWS, LANES), F32)]),
        out_shape=jax.ShapeDtypeStruct(gid.shape, jnp.int32),
        compiler_params=_params("arbitrary"),
        name="moe_slots",
    )(start, gid, before)


N_FILL = N_GROUPS + 1
SLAB = 8
assert SLAB * LANES == D_MODEL


def _rows_to_slabs(x, slab_ref):
    tm = x.shape[0]
    for k in range(SLAB):
        slab_ref[pl.ds(k, tm, stride=SLAB), :] = x[:, k * LANES:(k + 1) * LANES]


def _slabs_to_rows(slab_ref, tm):
    return jnp.concatenate([slab_ref[pl.ds(k, tm, stride=SLAB), :] for k in range(SLAB)], axis=1)


def _tile_of(r):
    return pl.ds(pl.multiple_of(r * SLAB, SLAB), SLAB)


TOKENS_PER_CHUNK = SLAB


def _chunk_rows(c):
    return pl.ds(pl.multiple_of(c * TOKENS_PER_CHUNK, TOKENS_PER_CHUNK), TOKENS_PER_CHUNK)


def _chunk_slab_rows(c, k):
    return pl.ds(c * TOKENS_PER_CHUNK * SLAB + k, TOKENS_PER_CHUNK, stride=SLAB)


def _start_chunk_copies(c, copy_of):
    for u in range(TOKENS_PER_CHUNK):
        copy_of(c * TOKENS_PER_CHUNK + u).start(priority=u % 2)


def _scatter_body(fill_ref, pos_ref, h_ref, hs_ref, slab_ref, zero_ref, sem, fill_sem):
    tm = h_ref.shape[0]
    i = pl.program_id(0)
    slot = i % 2

    def wait_slot(sl):
        pltpu.make_async_copy(slab_ref.at[sl], hs_ref.at[pl.ds(0, tm * SLAB)], sem.at[sl]).wait()

    @pl.when(i >= 2)
    def _():
        wait_slot(slot)

    def row_copy(t):
        return pltpu.make_async_copy(slab_ref.at[slot, _tile_of(t)], hs_ref.at[_tile_of(pos_ref[0, 0, t])],
                                     sem.at[slot])

    def chunk(c, carry):
        for k in range(SLAB):
            slab_ref[slot, _chunk_slab_rows(c, k), :] = h_ref[_chunk_rows(c), k * LANES:(k + 1) * LANES]
        _start_chunk_copies(c, row_copy)
        return carry
    lax.fori_loop(0, tm // TOKENS_PER_CHUNK, chunk, 0, unroll=2)

    @pl.when(i == 0)
    def _():
        zero_ref[...] = jnp.zeros_like(zero_ref)
        for f in range(N_FILL):
            def fill_copy(j, f=f):
                return pltpu.make_async_copy(zero_ref, hs_ref.at[_tile_of(fill_ref[f] + j)], fill_sem)

            def fill_start(j, carry, fill_copy=fill_copy):
                fill_copy(j).start()
                return carry

            def fill_wait(j, carry, fill_copy=fill_copy):
                fill_copy(j).wait()
                return carry
            lax.fori_loop(0, fill_ref[N_FILL + f], fill_start, 0)
            lax.fori_loop(0, fill_ref[N_FILL + f], fill_wait, 0)

    @pl.when(i == pl.num_programs(0) - 1)
    def _():
        @pl.when(i >= 1)
        def _():
            wait_slot(1 - slot)
        wait_slot(slot)


def _scatter_rows(fill, pos, h2, n_out):
    n_tiles, _, tm = pos.shape
    return pl.pallas_call(
        _scatter_body,
        grid_spec=pltpu.PrefetchScalarGridSpec(
            num_scalar_prefetch=1, grid=(n_tiles,),
            in_specs=[pl.BlockSpec((1, 1, tm), lambda i, *_: (i, 0, 0), memory_space=pltpu.SMEM),
                      pl.BlockSpec((tm, D_MODEL), lambda i, *_: (i, 0))],
            out_specs=pl.BlockSpec(memory_space=pl.ANY),
            scratch_shapes=[pltpu.VMEM((2, tm * SLAB, LANES), F32), pltpu.VMEM((SLAB, LANES), F32),
                            pltpu.SemaphoreType.DMA((2,)), pltpu.SemaphoreType.DMA]),
        out_shape=jax.ShapeDtypeStruct((n_out * SLAB, LANES), F32),
        compiler_params=_params("arbitrary"),
        name="moe_scatter_rows",
    )(fill, pos, h2)


def _gather_body(pos_ref, next_pos_ref, ys_ref, y_ref, slab_ref, sem):
    tm = y_ref.shape[0]
    i = pl.program_id(0)
    slot = i % 2
    n_chunks = tm // TOKENS_PER_CHUNK

    def fetch_copy(p_ref, sl):
        return lambda t: pltpu.make_async_copy(ys_ref.at[_tile_of(p_ref[0, 0, t])],
                                               slab_ref.at[sl, _tile_of(t)], sem.at[sl])

    def relayout(c):
        for k in range(SLAB):
            y_ref[_chunk_rows(c), k * LANES:(k + 1) * LANES] = slab_ref[slot, _chunk_slab_rows(c, k), :]

    @pl.when(i == 0)
    def _():
        def first(c, carry):
            _start_chunk_copies(c, fetch_copy(pos_ref, 0))
            return carry
        lax.fori_loop(0, n_chunks, first, 0)

    pltpu.make_async_copy(ys_ref.at[pl.ds(0, tm * SLAB)], slab_ref.at[slot], sem.at[slot]).wait()

    @pl.when(i + 1 < pl.num_programs(0))
    def _():
        def chunk(c, carry):
            _start_chunk_copies(c, fetch_copy(next_pos_ref, 1 - slot))
            relayout(c)
            return carry
        lax.fori_loop(0, n_chunks, chunk, 0, unroll=2)

    @pl.when(i + 1 == pl.num_programs(0))
    def _():
        def chunk(c, carry):
            relayout(c)
            return carry
        lax.fori_loop(0, n_chunks, chunk, 0, unroll=2)


def _gather_rows(pos, ys, n):
    n_tiles, _, tm = pos.shape
    smem = lambda index_map: pl.BlockSpec((1, 1, tm), index_map, memory_space=pltpu.SMEM)
    return pl.pallas_call(
        _gather_body,
        grid=(n_tiles,),
        in_specs=[smem(lambda i: (i, 0, 0)), smem(lambda i: (jnp.minimum(i + 1, n_tiles - 1), 0, 0)),
                  pl.BlockSpec(memory_space=pl.ANY)],
        out_specs=pl.BlockSpec((tm, D_MODEL), lambda i: (i, 0)),
        out_shape=jax.ShapeDtypeStruct((n, D_MODEL), F32),
        scratch_shapes=[pltpu.VMEM((2, tm * SLAB, LANES), F32), pltpu.SemaphoreType.DMA((2,))],
        compiler_params=_params("arbitrary"),
        name="moe_gather_rows",
    )(pos, pos, ys)


def _moe_body(tile_ref, h_ref, gm_ref, wr_ref, wg_ref, wu_ref, wd_ref, gf_ref, y_ref):
    tm = h_ref.shape[0] // SLAB
    n_used = tile_ref[0]
    grp = tile_ref[1 + jnp.minimum(pl.program_id(0), n_used - 1)]
    h = _slabs_to_rows(h_ref, tm)
    hn_hi, hn_lo = _split_bf16(_rms(h, gm_ref[...]))
    part = _dot(hn_hi, wr_ref[...]) + _dot(hn_lo, wr_ref[...])
    gates = _route(part[:, :LANES] + part[:, LANES:], grp)
    lane = lax.broadcasted_iota(jnp.int32, gates.shape, 1)
    first = lax.broadcasted_iota(jnp.int32, (tm, 2 * D_EXPERT), 1) < D_EXPERT
    acts = []
    for j in range(EXPERTS_PER_GROUP // 2):
        e0 = N_GROUPS + grp * EXPERTS_PER_GROUP + 2 * j
        g0 = jnp.sum(jnp.where(lane == e0, gates, 0.0), axis=-1, keepdims=True)
        g1 = jnp.sum(jnp.where(lane == e0 + 1, gates, 0.0), axis=-1, keepdims=True)
        hg = _dot(hn_hi, wg_ref[j])
        hu = _dot(hn_hi, wu_ref[j])
        acts.append((jax.nn.silu(hg) * hu * jnp.where(first, g0, g1)).astype(BF16))
    y = _rms(h + _dot(jnp.concatenate(acts, axis=1), wd_ref[0]), gf_ref[...])
    _rows_to_slabs(y, y_ref)


def _moe_sorted(tiles, hs, g_moe, wr, wg2, wu2, wd2, g_final):
    blk = TOKEN_TILE * SLAB
    n_steps = hs.shape[0] // blk
    ppg = EXPERTS_PER_GROUP // 2
    used = lambda s, tl: jnp.minimum(s, tl[0] - 1)
    grp = lambda s, tl: tl[1 + used(s, tl)]
    const = lambda shape: pl.BlockSpec(shape, lambda s, tl: (0,) * len(shape))
    return pl.pallas_call(
        _moe_body,
        grid_spec=pltpu.PrefetchScalarGridSpec(
            num_scalar_prefetch=1, grid=(n_steps,),
            in_specs=[pl.BlockSpec((blk, LANES), lambda s, tl: (used(s, tl), 0)), const((1, D_MODEL)),
                      const(wr.shape),
                      pl.BlockSpec((ppg, D_MODEL, 2 * D_EXPERT), lambda s, tl: (grp(s, tl), 0, 0)),
                      pl.BlockSpec((ppg, D_MODEL, 2 * D_EXPERT), lambda s, tl: (grp(s, tl), 0, 0)),
                      pl.BlockSpec((1, ppg * 2 * D_EXPERT, D_MODEL), lambda s, tl: (grp(s, tl), 0, 0)),
                      const((1, D_MODEL))],
            out_specs=pl.BlockSpec((blk, LANES), lambda s, tl: (s, 0))),
        out_shape=jax.ShapeDtypeStruct(hs.shape, F32),
        compiler_params=_params("arbitrary"),
        name="moe",
    )(tiles, hs, g_moe, wr, wg2, wu2, wd2, g_final)


def _moe(h2, w):
    n = h2.shape[0]
    tm = TOKEN_TILE
    n_steps = n // tm + N_GROUPS - 1
    gid, cnt = _group_of_tokens(h2, w["g_moe"], w["wrt"])
    cnt = cnt[:N_GROUPS, 0].astype(jnp.int32)
    padded = (cnt + tm - 1) // tm * tm
    end = jnp.cumsum(padded)
    start = end - padded
    tile_group = jnp.sum(jnp.arange(n_steps)[:, None] >= (end // tm)[None, :], axis=1)
    tiles = jnp.concatenate([end[-1:] // tm, jnp.minimum(tile_group, N_GROUPS - 1)]).astype(jnp.int32)
    fill = jnp.concatenate([start + cnt, end[-1:], padded - cnt, n_steps * tm - end[-1:]]).astype(jnp.int32)
    pos = _sorted_slots(start.astype(jnp.int32), gid, w["before"])
    hs = _scatter_rows(fill, pos, h2, n_steps * tm)
    ys = _moe_sorted(tiles, hs, w["g_moe"], w["wr"], w["wg2"], w["wu2"], w["wd2"], w["g_final"])
    return _gather_rows(pos, ys, n)


def _prep_weights(g_mix, w_in, g_sgu_v, sgu_w, sgu_b, na_rpb, g_out_a, g_out_b, w_out, g_xattn, g_mem,
                  w_xq, w_xkv, w_xo, g_moe, w_router_group, w_router_expert, w_exp_gate, w_exp_up,
                  w_exp_down, g_final):
    row = lambda g: g.reshape(1, -1).astype(F32)
    heads = sgu_w.shape[0]
    wcat = jnp.concatenate([sgu_w[0::2], sgu_w[1::2]], axis=2).astype(BF16)
    bexp = jnp.repeat(sgu_b.T, HEAD_DIM, axis=1).astype(F32)
    assert heads * HEAD_DIM == D_SGU
    wr = jnp.zeros((D_MODEL, LANES), F32)
    wr = wr.at[:, :N_GROUPS].set(w_router_group).at[:, N_GROUPS:N_GROUPS + N_EXPERTS].set(w_router_expert)
    wr_hi = wr.astype(BF16)
    wr_split = jnp.concatenate([wr_hi, (wr - wr_hi.astype(F32)).astype(BF16)], axis=1)
    before = jnp.triu(jnp.ones((TOKEN_TILE, TOKEN_TILE), BF16), k=1)
    pair_cols = lambda w: (w.reshape(N_EXPERTS // 2, 2, D_MODEL, D_EXPERT).transpose(0, 2, 1, 3)
                           .reshape(N_EXPERTS // 2, D_MODEL, 2 * D_EXPERT).astype(BF16))
    return dict(
        g_mix=row(g_mix), w_a=w_in[:, :2 * D_SGU].astype(BF16),
        w_qv=jnp.concatenate([w_in[:, 2 * D_SGU:2 * D_SGU + D_NA], w_in[:, 2 * D_SGU + 2 * D_NA:]], axis=1).astype(BF16),
        w_kt=w_in[:, 2 * D_SGU + D_NA:2 * D_SGU + 2 * D_NA].T.astype(BF16),
        g_sgu_v=row(g_sgu_v), wcat=wcat, bexp=bexp,
        bm=_na_bias_table(na_rpb), g_out_a=row(g_out_a), g_out_b=row(g_out_b), w_out=w_out.astype(BF16),
        g_xattn=row(g_xattn), g_mem=row(g_mem), w_xq=w_xq.astype(BF16), w_xkv=w_xkv.astype(BF16),
        w_xo=w_xo.astype(BF16), g_moe=row(g_moe), wr=wr_split, wrt=wr_split.T, before=before,
        wg2=pair_cols(w_exp_gate), wu2=pair_cols(w_exp_up),
        wd2=w_exp_down.reshape(N_GROUPS, EXPERTS_PER_GROUP * D_EXPERT, D_MODEL).astype(BF16),
        g_final=row(g_final))


def _trunk(x, mem, w):
    b, t, _ = x.shape
    km, vm = _mem_kv(mem, w["g_mem"], w["w_xkv"])
    q, kt, v, ya = _in_proj(x.reshape(b * t, D_MODEL), w["g_mix"], w["w_a"], w["w_qv"], w["w_kt"],
                            w["g_sgu_v"], w["wcat"], w["bexp"], w["g_out_a"])
    seq = lambda a: a.reshape(b, t, -1)
    yb = _na(seq(q), kt, seq(v), w["bm"])
    h2 = _mix_xattn(x, seq(ya), yb, km, vm, w["g_out_b"], w["w_out"], w["g_xattn"], w["w_xq"], w["w_xo"])
    y = _moe(h2.reshape(b * t, D_MODEL), w)
    return y.reshape(b, t, D_MODEL)


def kernel(x_prompt, x_sample, mem_prompt, mem_sample, g_mix, w_in, g_sgu_v, sgu_w, sgu_b, na_rpb, g_out_a,
           g_out_b, w_out, g_xattn, g_mem, w_xq, w_xkv, w_xo, g_moe, w_router_group, w_router_expert,
           w_exp_gate, w_exp_up, w_exp_down, g_final):
    assert g_mix.shape[0] == 1
    w = _prep_weights(g_mix[0], w_in[0], g_sgu_v[0], sgu_w[0], sgu_b[0], na_rpb[0], g_out_a[0], g_out_b[0],
                      w_out[0], g_xattn[0], g_mem[0], w_xq[0], w_xkv[0], w_xo[0], g_moe[0],
                      w_router_group[0], w_router_expert[0], w_exp_gate[0], w_exp_up[0], w_exp_down[0],
                      g_final)
    return (_trunk(x_prompt, mem_prompt, w), _trunk(x_sample, mem_sample, w))
```

```python
import functools
import math

import jax
import jax.numpy as jnp
import numpy as np
from jax import lax
from jax.experimental import pallas as pl
from jax.experimental.pallas import tpu as pltpu

F32 = jnp.float32
BF16 = jnp.bfloat16

D_MODEL = 1024
N_MEM = 256
GRID_W = 64
HEAD_DIM = 64
D_SGU = 512
D_NA = 512
CHUNK = 128
NA_KH = 8
NA_KW = 16
XA_HEADS = 4
XA_HEAD_DIM = 128
D_XA = 512
N_GROUPS = 4
EXPERTS_PER_GROUP = 8
N_EXPERTS = 32
D_EXPERT = 128
EPS = 1e-6

LANES = 128
NEG_BIG = -1e30
LOG2_E = math.log2(math.e)
TOKEN_TILE = 1024
NA_ROW_TILE = 8
VMEM_LIMIT_BYTES = 56 * 1024 * 1024


def _rms(x, g):
    return x * lax.rsqrt(jnp.mean(x * x, axis=-1, keepdims=True) + EPS) * g


def _dot(a, b):
    return jnp.dot(a, b, preferred_element_type=F32)


def _dot_nt(a, b):
    return lax.dot_general(a, b, (((1,), (1,)), ((), ())), preferred_element_type=F32)


def _whole(shape):
    nd = len(shape)
    return pl.BlockSpec(shape, lambda *_: (0,) * nd)


def _params(*sem, flags=None):
    return pltpu.CompilerParams(dimension_semantics=sem, vmem_limit_bytes=VMEM_LIMIT_BYTES, flags=flags)


def _mem_kv_body(mem_ref, g_ref, w_ref, k_ref, v_ref):
    memn = _rms(mem_ref[0], g_ref[...]).astype(BF16)
    kv = _dot(memn, w_ref[...])
    k_ref[0] = kv[:, :D_XA].astype(BF16)
    v_ref[0] = kv[:, D_XA:].astype(BF16)


def _mem_kv(mem, g_mem, w_xkv):
    b = mem.shape[0]
    out = jax.ShapeDtypeStruct((b, N_MEM, D_XA), BF16)
    return pl.pallas_call(
        _mem_kv_body,
        grid=(b,),
        in_specs=[pl.BlockSpec((1, N_MEM, D_MODEL), lambda i: (i, 0, 0)),
                  _whole((1, D_MODEL)), _whole((D_MODEL, 2 * D_XA))],
        out_specs=[pl.BlockSpec((1, N_MEM, D_XA), lambda i: (i, 0, 0))] * 2,
        out_shape=[out, out],
        compiler_params=_params("parallel"),
        name="mem_kv",
    )(mem, g_mem, w_xkv)


def _in_proj_body(x_ref, gmix_ref, wa_ref, wqv_ref, wkt_ref, gv_ref, wcat_ref, bexp_ref, goa_ref,
                  q_ref, kt_ref, v_ref, ya_ref):
    tm = x_ref.shape[0]
    hn = _rms(x_ref[...], gmix_ref[...]).astype(BF16)
    a = jax.nn.gelu(_dot(hn, wa_ref[...]))
    u = a[:, :D_SGU]
    vn = _rms(a[:, D_SGU:], gv_ref[...]).astype(BF16)
    lo = lax.broadcasted_iota(jnp.int32, (CHUNK, LANES), 1) < HEAD_DIM
    zero = jnp.zeros((CHUNK, LANES), BF16)
    chunks = []
    for c in range(tm // CHUNK):
        pairs = []
        for p in range(D_SGU // LANES):
            vp = vn[c * CHUNK:(c + 1) * CHUNK, p * LANES:(p + 1) * LANES]
            rhs = jnp.concatenate([jnp.where(lo, vp, zero), jnp.where(lo, zero, vp)], axis=0)
            pairs.append(_dot(wcat_ref[p], rhs))
        chunks.append(jnp.concatenate(pairs, axis=1) + bexp_ref[...])
    ya = u * jnp.concatenate(chunks, axis=0)
    ya_ref[...] = _rms(ya, goa_ref[...]).astype(BF16)
    qv = _dot(hn, wqv_ref[...])
    q_ref[...] = (qv[:, 0:D_NA] * (LOG2_E * HEAD_DIM ** -0.5)).astype(BF16)
    v_ref[...] = qv[:, D_NA:].astype(BF16)
    kt = _dot_nt(wkt_ref[...], hn).astype(BF16)
    for c in range(tm // LANES):
        kt_ref[c] = kt[:, c * LANES:(c + 1) * LANES]


def _in_proj(x2, g_mix, w_a, w_qv, w_kt, g_sgu_v, wcat, bexp, g_out_a):
    n = x2.shape[0]
    tm = TOKEN_TILE
    out = jax.ShapeDtypeStruct((n, D_NA), BF16)
    tok = lambda w: pl.BlockSpec((tm, w), lambda i: (i, 0))
    return pl.pallas_call(
        _in_proj_body,
        grid=(n // tm,),
        in_specs=[tok(D_MODEL), _whole((1, D_MODEL)), _whole(w_a.shape), _whole(w_qv.shape),
                  _whole(w_kt.shape), _whole((1, D_SGU)), _whole(wcat.shape), _whole(bexp.shape),
                  _whole((1, D_SGU))],
        out_specs=[tok(D_NA), pl.BlockSpec((tm // LANES, D_NA, LANES), lambda i: (i, 0, 0)),
                   tok(D_NA), tok(D_NA)],
        out_shape=[out, jax.ShapeDtypeStruct((n // LANES, D_NA, LANES), BF16), out, out],
        compiler_params=_params("parallel"),
        name="in_proj",
    )(x2, g_mix, w_a, w_qv, w_kt, g_sgu_v, wcat, bexp, g_out_a)


ROWS_PER_SLAB = LANES // GRID_W
SLABS_PER_WINDOW = NA_KH // ROWS_PER_SLAB


def _na_body(rows, q_ref, kt_ref, v_ref, bm_ref, o_ref, kk_ref, s_ref, p_ref):
    n_slabs = rows // ROWS_PER_SLAB
    n_tiles = rows // NA_ROW_TILE
    half = LANES // 2

    def shift(s, carry):
        a = kt_ref[s]
        kk_ref[0, s] = a
        kk_ref[1, s] = jnp.concatenate([a[:, half:], kt_ref[s + 1][:, :half]], axis=1)
        return carry
    lax.fori_loop(0, n_slabs - 1, shift, 0, unroll=8)
    last = kt_ref[n_slabs - 1]
    kk_ref[0, n_slabs - 1] = last
    kk_ref[1, n_slabs - 1] = last

    s_ref[1] = jnp.zeros(s_ref.shape[1:], F32)
    p_ref[0] = jnp.zeros(p_ref.shape[1:], BF16)

    lo = lax.broadcasted_iota(jnp.int32, (GRID_W, LANES), 1) < HEAD_DIM
    zero = jnp.zeros((GRID_W, LANES), BF16)

    def first_key_row(r):
        return jnp.clip(r - NA_KH // 2, 0, rows - NA_KH)

    def tok(r):
        return pl.multiple_of(r * GRID_W, GRID_W)

    def step(k, carry):
        slot_a = k % 2
        slot_b = (k + 1) % 2
        tile_a = jnp.minimum(k, n_tiles - 1)
        tile_b = jnp.clip(k - 1, 0, n_tiles - 1)
        tile_c = jnp.clip(k - 2, 0, n_tiles - 1)
        for i in range(NA_ROW_TILE):
            r = tile_c * NA_ROW_TILE + i
            vw = v_ref[0, pl.ds(tok(first_key_row(r)), NA_KH * GRID_W), :]
            o = _dot(p_ref[slot_a, i], vw)
            o_ref[0, pl.ds(tok(r), GRID_W), :] = jnp.where(lo, o[:GRID_W], o[GRID_W:]).astype(BF16)
        for i in range(NA_ROW_TILE):
            r = tile_b * NA_ROW_TILE + i
            s = s_ref[slot_b, i] + bm_ref[0, r - first_key_row(r)]
            e = jnp.exp2(s - jnp.max(s, axis=-1, keepdims=True))
            p_ref[slot_b, i] = (e / jnp.sum(e, axis=-1, keepdims=True)).astype(BF16)
        for i in range(NA_ROW_TILE):
            r = tile_a * NA_ROW_TILE + i
            rs = first_key_row(r)
            qrow = q_ref[0, pl.ds(tok(r), GRID_W), :]
            q2 = jnp.concatenate([jnp.where(lo, qrow, zero), jnp.where(lo, zero, qrow)], axis=0)
            odd = rs % ROWS_PER_SLAB
            s0 = rs // ROWS_PER_SLAB
            kw = jnp.concatenate([kk_ref[odd, s0 + j] for j in range(SLABS_PER_WINDOW)], axis=1)
            s_ref[slot_a, i] = _dot(q2, kw)
        return carry

    lax.fori_loop(0, n_tiles + 2, step, 0)


def _na(q, kt, v, bm):
    b, t, _ = q.shape
    rows = t // GRID_W
    assert rows >= 2 * NA_KH and rows % NA_ROW_TILE == 0
    n_pairs = D_NA // LANES
    n_slabs = t // LANES
    seq = pl.BlockSpec((1, t, LANES), lambda bi, p: (bi, 0, p))
    stage = (2, NA_ROW_TILE, 2 * GRID_W, NA_KH * GRID_W)
    return pl.pallas_call(
        functools.partial(_na_body, rows),
        grid=(b, n_pairs),
        in_specs=[seq, pl.BlockSpec((n_slabs, LANES, LANES), lambda bi, p: (bi, p, 0)), seq,
                  pl.BlockSpec((1, NA_KH, 2 * GRID_W, NA_KH * GRID_W), lambda bi, p: (p, 0, 0, 0))],
        out_specs=seq,
        out_shape=jax.ShapeDtypeStruct((b, t, D_NA), BF16),
        scratch_shapes=[pltpu.VMEM((2, n_slabs, LANES, LANES), BF16), pltpu.VMEM(stage, F32),
                        pltpu.VMEM(stage, BF16)],
        compiler_params=_params("parallel", "parallel"),
        name="na",
    )(q, kt, v, bm)


def _na_bias_table(rpb):
    c = np.arange(GRID_W)
    wstart = np.clip(c - NA_KW // 2, 0, GRID_W - NA_KW)
    kc = c[None, :]
    inwin = (kc >= wstart[:, None]) & (kc < wstart[:, None] + NA_KW)
    rel = np.clip(kc - c[:, None], -(NA_KW - 1), NA_KW - 1) + NA_KW - 1
    pick = jnp.asarray((rel[None] == np.arange(2 * NA_KW - 1)[:, None, None]).astype(np.float32))
    tab = jnp.einsum("hrk,kcn->hrcn", rpb * LOG2_E, pick, precision=lax.Precision.HIGHEST)
    tab = jnp.where(jnp.asarray(inwin)[None, None], tab, NEG_BIG)
    bmh = jnp.stack([tab[:, NA_KH - 1 - d:2 * NA_KH - 1 - d] for d in range(NA_KH)], axis=1)
    h = rpb.shape[0]
    bmh = bmh.transpose(0, 1, 3, 2, 4).reshape(h // 2, 2, NA_KH, GRID_W, NA_KH * GRID_W)
    return bmh.transpose(0, 2, 1, 3, 4).reshape(h // 2, NA_KH, 2 * GRID_W, NA_KH * GRID_W).astype(F32)


def _mix_xattn_body(x_ref, ya_ref, yb_ref, km_ref, vm_ref, gob_ref, wout_ref, gx_ref, wxq_ref,
                    wxo_ref, h2_ref):
    ybn = _rms(yb_ref[0].astype(F32), gob_ref[...]).astype(BF16)
    h1 = x_ref[0] + _dot(ya_ref[0], wout_ref[0:D_SGU, :]) + _dot(ybn, wout_ref[D_SGU:, :])
    hn = _rms(h1, gx_ref[...]).astype(BF16)
    q = _dot(hn, wxq_ref[...]).astype(BF16)
    heads = []
    for h in range(XA_HEADS):
        sl = slice(h * XA_HEAD_DIM, (h + 1) * XA_HEAD_DIM)
        s = _dot_nt(q[:, sl], km_ref[0, :, sl]) * (XA_HEAD_DIM ** -0.5)
        e = jnp.exp(s - jnp.max(s, axis=-1, keepdims=True))
        o = _dot(e.astype(BF16), vm_ref[0, :, sl]) / jnp.sum(e, axis=-1, keepdims=True)
        heads.append(o.astype(BF16))
    h2_ref[0] = h1 + _dot(jnp.concatenate(heads, axis=1), wxo_ref[...])


def _mix_xattn(x, ya, yb, km, vm, g_out_b, w_out, g_xattn, w_xq, w_xo):
    b, t, _ = x.shape
    tm = TOKEN_TILE
    tok = lambda w: pl.BlockSpec((1, tm, w), lambda bi, ti: (bi, ti, 0))
    mem = pl.BlockSpec((1, N_MEM, D_XA), lambda bi, ti: (bi, 0, 0))
    return pl.pallas_call(
        _mix_xattn_body,
        grid=(b, t // tm),
        in_specs=[tok(D_MODEL), tok(D_SGU), tok(D_NA), mem, mem, _whole((1, D_NA)),
                  _whole(w_out.shape), _whole((1, D_MODEL)), _whole(w_xq.shape), _whole(w_xo.shape)],
        out_specs=tok(D_MODEL),
        out_shape=jax.ShapeDtypeStruct((b, t, D_MODEL), F32),
        compiler_params=_params("parallel", "parallel"),
        name="mix_xattn",
    )(x, ya, yb, km, vm, g_out_b, w_out, g_xattn, w_xq, w_xo)


def _route(logits, grp):
    lane = lax.broadcasted_iota(jnp.int32, logits.shape, 1)
    lg = jnp.where(lane < N_GROUPS, logits, NEG_BIG)
    gmax = jnp.max(lg, axis=-1, keepdims=True)
    l_sel = jnp.sum(jnp.where(lane == grp, logits, 0.0), axis=-1, keepdims=True)
    g_top = jnp.exp(l_sel - gmax) / jnp.sum(jnp.exp(lg - gmax), axis=-1, keepdims=True)
    first = N_GROUPS + grp * EXPERTS_PER_GROUP
    le = jnp.where((lane >= first) & (lane < first + EXPERTS_PER_GROUP), logits, NEG_BIG)
    m1 = jnp.max(le, axis=-1, keepdims=True)
    i1 = jnp.min(jnp.where(le == m1, lane, LANES), axis=-1, keepdims=True)
    le2 = jnp.where(lane == i1, NEG_BIG, le)
    m2 = jnp.max(le2, axis=-1, keepdims=True)
    i2 = jnp.min(jnp.where(le2 == m2, lane, LANES), axis=-1, keepdims=True)
    p2 = jnp.exp(m2 - m1)
    w1 = g_top / (1.0 + p2)
    w2 = g_top * p2 / (1.0 + p2)
    return jnp.where(lane == i1, w1, 0.0) + jnp.where(lane == i2, w2, 0.0)


def _split_bf16(x):
    hi = x.astype(BF16)
    return hi, (x - hi.astype(F32)).astype(BF16)


GROUP_ROWS = 8


def _group_body(h_ref, gm_ref, wrt_ref, gid_ref, cnt_ref):
    hn_hi, hn_lo = _split_bf16(_rms(h_ref[...], gm_ref[...]))
    part = _dot_nt(wrt_ref[...], hn_hi) + _dot_nt(wrt_ref[...], hn_lo)
    logit_t = part[:LANES] + part[LANES:]
    row = lax.broadcasted_iota(jnp.int32, (GROUP_ROWS, logit_t.shape[1]), 0)
    lg = jnp.where(row < N_GROUPS, logit_t[:GROUP_ROWS], NEG_BIG)
    gmax = jnp.max(lg, axis=0, keepdims=True)
    gid = jnp.min(jnp.where(lg == gmax, row, GROUP_ROWS), axis=0, keepdims=True)
    gid_ref[0] = gid

    @pl.when(pl.program_id(0) == 0)
    def _():
        cnt_ref[...] = jnp.zeros_like(cnt_ref)
    cnt_ref[...] += jnp.sum((row == gid).astype(F32), axis=1, keepdims=True)


def _group_of_tokens(h2, g_moe, wrt):
    n = h2.shape[0]
    tm = TOKEN_TILE
    return pl.pallas_call(
        _group_body,
        grid=(n // tm,),
        in_specs=[pl.BlockSpec((tm, D_MODEL), lambda i: (i, 0)), _whole((1, D_MODEL)), _whole(wrt.shape)],
        out_specs=[pl.BlockSpec((1, 1, tm), lambda i: (i, 0, 0)), _whole((GROUP_ROWS, LANES))],
        out_shape=[jax.ShapeDtypeStruct((n // tm, 1, tm), jnp.int32),
                   jax.ShapeDtypeStruct((GROUP_ROWS, LANES), F32)],
        compiler_params=_params("arbitrary"),
        name="moe_group",
    )(h2, g_moe, wrt)


def _slot_body(start_ref, gid_ref, before_ref, pos_ref, run_ref):
    @pl.when(pl.program_id(0) == 0)
    def _():
        run_ref[...] = jnp.zeros_like(run_ref)
    gid = gid_ref[0]
    row = lax.broadcasted_iota(jnp.int32, (GROUP_ROWS, gid.shape[1]), 0)
    onehot = row == gid
    earlier = _dot(onehot.astype(BF16), before_ref[...])
    row1 = lax.broadcasted_iota(jnp.int32, (GROUP_ROWS, 1), 0)
    start = jnp.zeros((GROUP_ROWS, 1), F32)
    for g in range(N_GROUPS):
        start = jnp.where(row1 == g, start_ref[g].astype(F32), start)
    slot = jnp.sum(jnp.where(onehot, earlier + (start + run_ref[:, 0:1]), 0.0), axis=0, keepdims=True)
    pos_ref[0] = slot.astype(jnp.int32)
    run_ref[...] += jnp.sum(onehot.astype(F32), axis=1, keepdims=True)


def _sorted_slots(start, gid, before):
    n_tiles, _, tm = gid.shape
    blk = pl.BlockSpec((1, 1, tm), lambda i, *_: (i, 0, 0))
    return pl.pallas_call(
        _slot_body,
        grid_spec=pltpu.PrefetchScalarGridSpec(
            num_scalar_prefetch=1, grid=(n_tiles,),
            in_specs=[blk, pl.BlockSpec(before.shape, lambda i, *_: (0, 0))], out_specs=blk,
            scratch_shapes=[pltpu.VMEM((GROUP_ROWS, LANES), F32)]),
        out_shape=jax.ShapeDtypeStruct(gid.shape, jnp.int32),
        compiler_params=_params("arbitrary"),
        name="moe_slots",
    )(start, gid, before)


N_FILL = N_GROUPS + 1
SLAB = 8
assert SLAB * LANES == D_MODEL


def _rows_to_slabs(x, slab_ref):
    tm = x.shape[0]
    for k in range(SLAB):
        slab_ref[pl.ds(k, tm, stride=SLAB), :] = x[:, k * LANES:(k + 1) * LANES]


def _slabs_to_rows(slab_ref, tm):
    return jnp.concatenate([slab_ref[pl.ds(k, tm, stride=SLAB), :] for k in range(SLAB)], axis=1)


def _tile_of(r):
    return pl.ds(pl.multiple_of(r * SLAB, SLAB), SLAB)


TOKENS_PER_CHUNK = SLAB


def _chunk_rows(c):
    return pl.ds(pl.multiple_of(c * TOKENS_PER_CHUNK, TOKENS_PER_CHUNK), TOKENS_PER_CHUNK)


def _chunk_slab_rows(c, k):
    return pl.ds(c * TOKENS_PER_CHUNK * SLAB + k, TOKENS_PER_CHUNK, stride=SLAB)


def _start_chunk_copies(c, copy_of):
    for u in range(TOKENS_PER_CHUNK):
        copy_of(c * TOKENS_PER_CHUNK + u).start(priority=u % 2)


def _scatter_body(fill_ref, pos_ref, h_ref, hs_ref, slab_ref, zero_ref, sem, fill_sem):
    tm = h_ref.shape[0]
    i = pl.program_id(0)
    slot = i % 2

    def wait_slot(sl):
        pltpu.make_async_copy(slab_ref.at[sl], hs_ref.at[pl.ds(0, tm * SLAB)], sem.at[sl]).wait()

    @pl.when(i >= 2)
    def _():
        wait_slot(slot)

    def row_copy(t):
        return pltpu.make_async_copy(slab_ref.at[slot, _tile_of(t)], hs_ref.at[_tile_of(pos_ref[0, 0, t])],
                                     sem.at[slot])

    def chunk(c, carry):
        for k in range(SLAB):
            slab_ref[slot, _chunk_slab_rows(c, k), :] = h_ref[_chunk_rows(c), k * LANES:(k + 1) * LANES]
        _start_chunk_copies(c, row_copy)
        return carry
    lax.fori_loop(0, tm // TOKENS_PER_CHUNK, chunk, 0, unroll=2)

    @pl.when(i == 0)
    def _():
        zero_ref[...] = jnp.zeros_like(zero_ref)
        for f in range(N_FILL):
            def fill_copy(j, f=f):
                return pltpu.make_async_copy(zero_ref, hs_ref.at[_tile_of(fill_ref[f] + j)], fill_sem)

            def fill_start(j, carry, fill_copy=fill_copy):
                fill_copy(j).start()
                return carry

            def fill_wait(j, carry, fill_copy=fill_copy):
                fill_copy(j).wait()
                return carry
            lax.fori_loop(0, fill_ref[N_FILL + f], fill_start, 0)
            lax.fori_loop(0, fill_ref[N_FILL + f], fill_wait, 0)

    @pl.when(i == pl.num_programs(0) - 1)
    def _():
        @pl.when(i >= 1)
        def _():
            wait_slot(1 - slot)
        wait_slot(slot)


def _scatter_rows(fill, pos, h2, n_out):
    n_tiles, _, tm = pos.shape
    return pl.pallas_call(
        _scatter_body,
        grid_spec=pltpu.PrefetchScalarGridSpec(
            num_scalar_prefetch=1, grid=(n_tiles,),
            in_specs=[pl.BlockSpec((1, 1, tm), lambda i, *_: (i, 0, 0), memory_space=pltpu.SMEM),
                      pl.BlockSpec((tm, D_MODEL), lambda i, *_: (i, 0))],
            out_specs=pl.BlockSpec(memory_space=pl.ANY),
            scratch_shapes=[pltpu.VMEM((2, tm * SLAB, LANES), F32), pltpu.VMEM((SLAB, LANES), F32),
                            pltpu.SemaphoreType.DMA((2,)), pltpu.SemaphoreType.DMA]),
        out_shape=jax.ShapeDtypeStruct((n_out * SLAB, LANES), F32),
        compiler_params=_params("arbitrary"),
        name="moe_scatter_rows",
    )(fill, pos, h2)


def _gather_body(pos_ref, next_pos_ref, ys_ref, y_ref, slab_ref, sem):
    tm = y_ref.shape[0]
    i = pl.program_id(0)
    slot = i % 2

    def fetch(p_ref, sl):
        def chunk(c, carry):
            _start_chunk_copies(c, lambda t: pltpu.make_async_copy(
                ys_ref.at[_tile_of(p_ref[0, 0, t])], slab_ref.at[sl, _tile_of(t)], sem.at[sl]))
            return carry
        lax.fori_loop(0, tm // TOKENS_PER_CHUNK, chunk, 0)

    @pl.when(i == 0)
    def _():
        fetch(pos_ref, 0)

    @pl.when(i + 1 < pl.num_programs(0))
    def _():
        fetch(next_pos_ref, 1 - slot)

    pltpu.make_async_copy(ys_ref.at[pl.ds(0, tm * SLAB)], slab_ref.at[slot], sem.at[slot]).wait()
    y_ref[...] = _slabs_to_rows(slab_ref.at[slot], tm)


def _gather_rows(pos, ys, n):
    n_tiles, _, tm = pos.shape
    smem = lambda index_map: pl.BlockSpec((1, 1, tm), index_map, memory_space=pltpu.SMEM)
    return pl.pallas_call(
        _gather_body,
        grid=(n_tiles,),
        in_specs=[smem(lambda i: (i, 0, 0)), smem(lambda i: (jnp.minimum(i + 1, n_tiles - 1), 0, 0)),
                  pl.BlockSpec(memory_space=pl.ANY)],
        out_specs=pl.BlockSpec((tm, D_MODEL), lambda i: (i, 0)),
        out_shape=jax.ShapeDtypeStruct((n, D_MODEL), F32),
        scratch_shapes=[pltpu.VMEM((2, tm * SLAB, LANES), F32), pltpu.SemaphoreType.DMA((2,))],
        compiler_params=_params("arbitrary"),
        name="moe_gather_rows",
    )(pos, pos, ys)


def _moe_body(tile_ref, h_ref, gm_ref, wr_ref, wg_ref, wu_ref, wd_ref, gf_ref, y_ref):
    tm = h_ref.shape[0] // SLAB
    n_used = tile_ref[0]
    grp = tile_ref[1 + jnp.minimum(pl.program_id(0), n_used - 1)]
    h = _slabs_to_rows(h_ref, tm)
    hn_hi, hn_lo = _split_bf16(_rms(h, gm_ref[...]))
    part = _dot(hn_hi, wr_ref[...]) + _dot(hn_lo, wr_ref[...])
    gates = _route(part[:, :LANES] + part[:, LANES:], grp)
    lane = lax.broadcasted_iota(jnp.int32, gates.shape, 1)
    first = lax.broadcasted_iota(jnp.int32, (tm, 2 * D_EXPERT), 1) < D_EXPERT
    acts = []
    for j in range(EXPERTS_PER_GROUP // 2):
        e0 = N_GROUPS + grp * EXPERTS_PER_GROUP + 2 * j
        g0 = jnp.sum(jnp.where(lane == e0, gates, 0.0), axis=-1, keepdims=True)
        g1 = jnp.sum(jnp.where(lane == e0 + 1, gates, 0.0), axis=-1, keepdims=True)
        hg = _dot(hn_hi, wg_ref[j])
        hu = _dot(hn_hi, wu_ref[j])
        acts.append((jax.nn.silu(hg) * hu * jnp.where(first, g0, g1)).astype(BF16))
    y = _rms(h + _dot(jnp.concatenate(acts, axis=1), wd_ref[0]), gf_ref[...])
    _rows_to_slabs(y, y_ref)


def _moe_sorted(tiles, hs, g_moe, wr, wg2, wu2, wd2, g_final):
    blk = TOKEN_TILE * SLAB
    n_steps = hs.shape[0] // blk
    ppg = EXPERTS_PER_GROUP // 2
    used = lambda s, tl: jnp.minimum(s, tl[0] - 1)
    grp = lambda s, tl: tl[1 + used(s, tl)]
    const = lambda shape: pl.BlockSpec(shape, lambda s, tl: (0,) * len(shape))
    return pl.pallas_call(
        _moe_body,
        grid_spec=pltpu.PrefetchScalarGridSpec(
            num_scalar_prefetch=1, grid=(n_steps,),
            in_specs=[pl.BlockSpec((blk, LANES), lambda s, tl: (used(s, tl), 0)), const((1, D_MODEL)),
                      const(wr.shape),
                      pl.BlockSpec((ppg, D_MODEL, 2 * D_EXPERT), lambda s, tl: (grp(s, tl), 0, 0)),
                      pl.BlockSpec((ppg, D_MODEL, 2 * D_EXPERT), lambda s, tl: (grp(s, tl), 0, 0)),
                      pl.BlockSpec((1, ppg * 2 * D_EXPERT, D_MODEL), lambda s, tl: (grp(s, tl), 0, 0)),
                      const((1, D_MODEL))],
            out_specs=pl.BlockSpec((blk, LANES), lambda s, tl: (s, 0))),
        out_shape=jax.ShapeDtypeStruct(hs.shape, F32),
        compiler_params=_params("arbitrary"),
        name="moe",
    )(tiles, hs, g_moe, wr, wg2, wu2, wd2, g_final)


def _moe(h2, w):
    n = h2.shape[0]
    tm = TOKEN_TILE
    n_steps = n // tm + N_GROUPS - 1
    gid, cnt = _group_of_tokens(h2, w["g_moe"], w["wrt"])
    cnt = cnt[:N_GROUPS, 0].astype(jnp.int32)
    padded = (cnt + tm - 1) // tm * tm
    end = jnp.cumsum(padded)
    start = end - padded
    tile_group = jnp.sum(jnp.arange(n_steps)[:, None] >= (end // tm)[None, :], axis=1)
    tiles = jnp.concatenate([end[-1:] // tm, jnp.minimum(tile_group, N_GROUPS - 1)]).astype(jnp.int32)
    fill = jnp.concatenate([start + cnt, end[-1:], padded - cnt, n_steps * tm - end[-1:]]).astype(jnp.int32)
    pos = _sorted_slots(start.astype(jnp.int32), gid, w["before"])
    hs = _scatter_rows(fill, pos, h2, n_steps * tm)
    ys = _moe_sorted(tiles, hs, w["g_moe"], w["wr"], w["wg2"], w["wu2"], w["wd2"], w["g_final"])
    return _gather_rows(pos, ys, n)


def _prep_weights(g_mix, w_in, g_sgu_v, sgu_w, sgu_b, na_rpb, g_out_a, g_out_b, w_out, g_xattn, g_mem,
                  w_xq, w_xkv, w_xo, g_moe, w_router_group, w_router_expert, w_exp_gate, w_exp_up,
                  w_exp_down, g_final):
    row = lambda g: g.reshape(1, -1).astype(F32)
    heads = sgu_w.shape[0]
    wcat = jnp.concatenate([sgu_w[0::2], sgu_w[1::2]], axis=2).astype(BF16)
    bexp = jnp.repeat(sgu_b.T, HEAD_DIM, axis=1).astype(F32)
    assert heads * HEAD_DIM == D_SGU
    wr = jnp.zeros((D_MODEL, LANES), F32)
    wr = wr.at[:, :N_GROUPS].set(w_router_group).at[:, N_GROUPS:N_GROUPS + N_EXPERTS].set(w_router_expert)
    wr_hi = wr.astype(BF16)
    wr_split = jnp.concatenate([wr_hi, (wr - wr_hi.astype(F32)).astype(BF16)], axis=1)
    before = jnp.triu(jnp.ones((TOKEN_TILE, TOKEN_TILE), BF16), k=1)
    pair_cols = lambda w: (w.reshape(N_EXPERTS // 2, 2, D_MODEL, D_EXPERT).transpose(0, 2, 1, 3)
                           .reshape(N_EXPERTS // 2, D_MODEL, 2 * D_EXPERT).astype(BF16))
    return dict(
        g_mix=row(g_mix), w_a=w_in[:, :2 * D_SGU].astype(BF16),
        w_qv=jnp.concatenate([w_in[:, 2 * D_SGU:2 * D_SGU + D_NA], w_in[:, 2 * D_SGU + 2 * D_NA:]], axis=1).astype(BF16),
        w_kt=w_in[:, 2 * D_SGU + D_NA:2 * D_SGU + 2 * D_NA].T.astype(BF16),
        g_sgu_v=row(g_sgu_v), wcat=wcat, bexp=bexp,
        bm=_na_bias_table(na_rpb), g_out_a=row(g_out_a), g_out_b=row(g_out_b), w_out=w_out.astype(BF16),
        g_xattn=row(g_xattn), g_mem=row(g_mem), w_xq=w_xq.astype(BF16), w_xkv=w_xkv.astype(BF16),
        w_xo=w_xo.astype(BF16), g_moe=row(g_moe), wr=wr_split, wrt=wr_split.T, before=before,
        wg2=pair_cols(w_exp_gate), wu2=pair_cols(w_exp_up),
        wd2=w_exp_down.reshape(N_GROUPS, EXPERTS_PER_GROUP * D_EXPERT, D_MODEL).astype(BF16),
        g_final=row(g_final))


def _trunk(x, mem, w):
    b, t, _ = x.shape
    km, vm = _mem_kv(mem, w["g_mem"], w["w_xkv"])
    q, kt, v, ya = _in_proj(x.reshape(b * t, D_MODEL), w["g_mix"], w["w_a"], w["w_qv"], w["w_kt"],
                            w["g_sgu_v"], w["wcat"], w["bexp"], w["g_out_a"])
    seq = lambda a: a.reshape(b, t, -1)
    yb = _na(seq(q), kt, seq(v), w["bm"])
    h2 = _mix_xattn(x, seq(ya), yb, km, vm, w["g_out_b"], w["w_out"], w["g_xattn"], w["w_xq"], w["w_xo"])
    y = _moe(h2.reshape(b * t, D_MODEL), w)
    return y.reshape(b, t, D_MODEL)


def kernel(x_prompt, x_sample, mem_prompt, mem_sample, g_mix, w_in, g_sgu_v, sgu_w, sgu_b, na_rpb, g_out_a,
           g_out_b, w_out, g_xattn, g_mem, w_xq, w_xkv, w_xo, g_moe, w_router_group, w_router_expert,
           w_exp_gate, w_exp_up, w_exp_down, g_final):
    assert g_mix.shape[0] == 1
    w = _prep_weights(g_mix[0], w_in[0], g_sgu_v[0], sgu_w[0], sgu_b[0], na_rpb[0], g_out_a[0], g_out_b[0],
                      w_out[0], g_xattn[0], g_mem[0], w_xq[0], w_xkv[0], w_xo[0], g_moe[0],
                      w_router_group[0], w_router_expert[0], w_exp_gate[0], w_exp_up[0], w_exp_down[0],
                      g_final)
    return (_trunk(x_prompt, mem_prompt, w), _trunk(x_sample, mem_sample, w))
```

```python
import functools
import math

import jax
import jax.numpy as jnp
import numpy as np
from jax import lax
from jax.experimental import pallas as pl
from jax.experimental.pallas import tpu as pltpu

F32 = jnp.float32
BF16 = jnp.bfloat16

D_MODEL = 1024
N_MEM = 256
GRID_W = 64
HEAD_DIM = 64
D_SGU = 512
D_NA = 512
CHUNK = 128
NA_KH = 8
NA_KW = 16
XA_HEADS = 4
XA_HEAD_DIM = 128
D_XA = 512
N_GROUPS = 4
EXPERTS_PER_GROUP = 8
N_EXPERTS = 32
D_EXPERT = 128
EPS = 1e-6

LANES = 128
NEG_BIG = -1e30
LOG2_E = math.log2(math.e)
TOKEN_TILE = 1024
NA_ROW_TILE = 8
VMEM_LIMIT_BYTES = 56 * 1024 * 1024


def _rms(x, g):
    return x * lax.rsqrt(jnp.mean(x * x, axis=-1, keepdims=True) + EPS) * g


def _dot(a, b):
    return jnp.dot(a, b, preferred_element_type=F32)


def _dot_nt(a, b):
    return lax.dot_general(a, b, (((1,), (1,)), ((), ())), preferred_element_type=F32)


def _whole(shape):
    nd = len(shape)
    return pl.BlockSpec(shape, lambda *_: (0,) * nd)


def _params(*sem, flags=None):
    return pltpu.CompilerParams(dimension_semantics=sem, vmem_limit_bytes=VMEM_LIMIT_BYTES, flags=flags)


def _mem_kv_body(mem_ref, g_ref, w_ref, k_ref, v_ref):
    memn = _rms(mem_ref[0], g_ref[...]).astype(BF16)
    kv = _dot(memn, w_ref[...])
    k_ref[0] = kv[:, :D_XA].astype(BF16)
    v_ref[0] = kv[:, D_XA:].astype(BF16)


def _mem_kv(mem, g_mem, w_xkv):
    b = mem.shape[0]
    out = jax.ShapeDtypeStruct((b, N_MEM, D_XA), BF16)
    return pl.pallas_call(
        _mem_kv_body,
        grid=(b,),
        in_specs=[pl.BlockSpec((1, N_MEM, D_MODEL), lambda i: (i, 0, 0)),
                  _whole((1, D_MODEL)), _whole((D_MODEL, 2 * D_XA))],
        out_specs=[pl.BlockSpec((1, N_MEM, D_XA), lambda i: (i, 0, 0))] * 2,
        out_shape=[out, out],
        compiler_params=_params("parallel"),
        name="mem_kv",
    )(mem, g_mem, w_xkv)


def _in_proj_body(x_ref, gmix_ref, wa_ref, wqv_ref, wkt_ref, gv_ref, wcat_ref, bexp_ref, goa_ref,
                  q_ref, kt_ref, v_ref, ya_ref):
    tm = x_ref.shape[0]
    hn = _rms(x_ref[...], gmix_ref[...]).astype(BF16)
    a = jax.nn.gelu(_dot(hn, wa_ref[...]))
    u = a[:, :D_SGU]
    vn = _rms(a[:, D_SGU:], gv_ref[...]).astype(BF16)
    lo = lax.broadcasted_iota(jnp.int32, (CHUNK, LANES), 1) < HEAD_DIM
    zero = jnp.zeros((CHUNK, LANES), BF16)
    chunks = []
    for c in range(tm // CHUNK):
        pairs = []
        for p in range(D_SGU // LANES):
            vp = vn[c * CHUNK:(c + 1) * CHUNK, p * LANES:(p + 1) * LANES]
            rhs = jnp.concatenate([jnp.where(lo, vp, zero), jnp.where(lo, zero, vp)], axis=0)
            pairs.append(_dot(wcat_ref[p], rhs))
        chunks.append(jnp.concatenate(pairs, axis=1) + bexp_ref[...])
    ya = u * jnp.concatenate(chunks, axis=0)
    ya_ref[...] = _rms(ya, goa_ref[...]).astype(BF16)
    qv = _dot(hn, wqv_ref[...])
    q_ref[...] = (qv[:, 0:D_NA] * (LOG2_E * HEAD_DIM ** -0.5)).astype(BF16)
    v_ref[...] = qv[:, D_NA:].astype(BF16)
    kt = _dot_nt(wkt_ref[...], hn).astype(BF16)
    for c in range(tm // LANES):
        kt_ref[c] = kt[:, c * LANES:(c + 1) * LANES]


def _in_proj(x2, g_mix, w_a, w_qv, w_kt, g_sgu_v, wcat, bexp, g_out_a):
    n = x2.shape[0]
    tm = TOKEN_TILE
    out = jax.ShapeDtypeStruct((n, D_NA), BF16)
    tok = lambda w: pl.BlockSpec((tm, w), lambda i: (i, 0))
    return pl.pallas_call(
        _in_proj_body,
        grid=(n // tm,),
        in_specs=[tok(D_MODEL), _whole((1, D_MODEL)), _whole(w_a.shape), _whole(w_qv.shape),
                  _whole(w_kt.shape), _whole((1, D_SGU)), _whole(wcat.shape), _whole(bexp.shape),
                  _whole((1, D_SGU))],
        out_specs=[tok(D_NA), pl.BlockSpec((tm // LANES, D_NA, LANES), lambda i: (i, 0, 0)),
                   tok(D_NA), tok(D_NA)],
        out_shape=[out, jax.ShapeDtypeStruct((n // LANES, D_NA, LANES), BF16), out, out],
        compiler_params=_params("parallel"),
        name="in_proj",
    )(x2, g_mix, w_a, w_qv, w_kt, g_sgu_v, wcat, bexp, g_out_a)


ROWS_PER_SLAB = LANES // GRID_W
SLABS_PER_WINDOW = NA_KH // ROWS_PER_SLAB


def _na_body(rows, q_ref, kt_ref, v_ref, bm_ref, o_ref, kk_ref, s_ref, p_ref):
    n_slabs = rows // ROWS_PER_SLAB
    n_tiles = rows // NA_ROW_TILE
    half = LANES // 2

    def shift(s, carry):
        a = kt_ref[s]
        kk_ref[0, s] = a
        kk_ref[1, s] = jnp.concatenate([a[:, half:], kt_ref[s + 1][:, :half]], axis=1)
        return carry
    lax.fori_loop(0, n_slabs - 1, shift, 0, unroll=8)
    last = kt_ref[n_slabs - 1]
    kk_ref[0, n_slabs - 1] = last
    kk_ref[1, n_slabs - 1] = last

    s_ref[1] = jnp.zeros(s_ref.shape[1:], F32)
    p_ref[0] = jnp.zeros(p_ref.shape[1:], BF16)

    lo = lax.broadcasted_iota(jnp.int32, (GRID_W, LANES), 1) < HEAD_DIM
    zero = jnp.zeros((GRID_W, LANES), BF16)

    def first_key_row(r):
        return jnp.clip(r - NA_KH // 2, 0, rows - NA_KH)

    def tok(r):
        return pl.multiple_of(r * GRID_W, GRID_W)

    def step(k, carry):
        slot_a = k % 2
        slot_b = (k + 1) % 2
        tile_a = jnp.minimum(k, n_tiles - 1)
        tile_b = jnp.clip(k - 1, 0, n_tiles - 1)
        tile_c = jnp.clip(k - 2, 0, n_tiles - 1)
        for i in range(NA_ROW_TILE):
            r = tile_c * NA_ROW_TILE + i
            vw = v_ref[0, pl.ds(tok(first_key_row(r)), NA_KH * GRID_W), :]
            o = _dot(p_ref[slot_a, i], vw)
            o_ref[0, pl.ds(tok(r), GRID_W), :] = jnp.where(lo, o[:GRID_W], o[GRID_W:]).astype(BF16)
        for i in range(NA_ROW_TILE):
            r = tile_b * NA_ROW_TILE + i
            s = s_ref[slot_b, i] + bm_ref[0, r - first_key_row(r)]
            e = jnp.exp2(s - jnp.max(s, axis=-1, keepdims=True))
            p_ref[slot_b, i] = (e / jnp.sum(e, axis=-1, keepdims=True)).astype(BF16)
        for i in range(NA_ROW_TILE):
            r = tile_a * NA_ROW_TILE + i
            rs = first_key_row(r)
            qrow = q_ref[0, pl.ds(tok(r), GRID_W), :]
            q2 = jnp.concatenate([jnp.where(lo, qrow, zero), jnp.where(lo, zero, qrow)], axis=0)
            odd = rs % ROWS_PER_SLAB
            s0 = rs // ROWS_PER_SLAB
            kw = jnp.concatenate([kk_ref[odd, s0 + j] for j in range(SLABS_PER_WINDOW)], axis=1)
            s_ref[slot_a, i] = _dot(q2, kw)
        return carry

    lax.fori_loop(0, n_tiles + 2, step, 0)


def _na(q, kt, v, bm):
    b, t, _ = q.shape
    rows = t // GRID_W
    assert rows >= 2 * NA_KH and rows % NA_ROW_TILE == 0
    n_pairs = D_NA // LANES
    n_slabs = t // LANES
    seq = pl.BlockSpec((1, t, LANES), lambda bi, p: (bi, 0, p))
    stage = (2, NA_ROW_TILE, 2 * GRID_W, NA_KH * GRID_W)
    return pl.pallas_call(
        functools.partial(_na_body, rows),
        grid=(b, n_pairs),
        in_specs=[seq, pl.BlockSpec((n_slabs, LANES, LANES), lambda bi, p: (bi, p, 0)), seq,
                  pl.BlockSpec((1, NA_KH, 2 * GRID_W, NA_KH * GRID_W), lambda bi, p: (p, 0, 0, 0))],
        out_specs=seq,
        out_shape=jax.ShapeDtypeStruct((b, t, D_NA), BF16),
        scratch_shapes=[pltpu.VMEM((2, n_slabs, LANES, LANES), BF16), pltpu.VMEM(stage, F32),
                        pltpu.VMEM(stage, BF16)],
        compiler_params=_params("parallel", "parallel"),
        name="na",
    )(q, kt, v, bm)


def _na_bias_table(rpb):
    c = np.arange(GRID_W)
    wstart = np.clip(c - NA_KW // 2, 0, GRID_W - NA_KW)
    kc = c[None, :]
    inwin = (kc >= wstart[:, None]) & (kc < wstart[:, None] + NA_KW)
    rel = np.clip(kc - c[:, None], -(NA_KW - 1), NA_KW - 1) + NA_KW - 1
    pick = jnp.asarray((rel[None] == np.arange(2 * NA_KW - 1)[:, None, None]).astype(np.float32))
    tab = jnp.einsum("hrk,kcn->hrcn", rpb * LOG2_E, pick, precision=lax.Precision.HIGHEST)
    tab = jnp.where(jnp.asarray(inwin)[None, None], tab, NEG_BIG)
    bmh = jnp.stack([tab[:, NA_KH - 1 - d:2 * NA_KH - 1 - d] for d in range(NA_KH)], axis=1)
    h = rpb.shape[0]
    bmh = bmh.transpose(0, 1, 3, 2, 4).reshape(h // 2, 2, NA_KH, GRID_W, NA_KH * GRID_W)
    return bmh.transpose(0, 2, 1, 3, 4).reshape(h // 2, NA_KH, 2 * GRID_W, NA_KH * GRID_W).astype(F32)


def _mix_xattn_body(x_ref, ya_ref, yb_ref, km_ref, vm_ref, gob_ref, wout_ref, gx_ref, wxq_ref,
                    wxo_ref, h2_ref):
    ybn = _rms(yb_ref[0].astype(F32), gob_ref[...]).astype(BF16)
    h1 = x_ref[0] + _dot(ya_ref[0], wout_ref[0:D_SGU, :]) + _dot(ybn, wout_ref[D_SGU:, :])
    hn = _rms(h1, gx_ref[...]).astype(BF16)
    q = _dot(hn, wxq_ref[...]).astype(BF16)
    heads = []
    for h in range(XA_HEADS):
        sl = slice(h * XA_HEAD_DIM, (h + 1) * XA_HEAD_DIM)
        s = _dot_nt(q[:, sl], km_ref[0, :, sl]) * (XA_HEAD_DIM ** -0.5)
        e = jnp.exp(s - jnp.max(s, axis=-1, keepdims=True))
        o = _dot(e.astype(BF16), vm_ref[0, :, sl]) / jnp.sum(e, axis=-1, keepdims=True)
        heads.append(o.astype(BF16))
    h2_ref[0] = h1 + _dot(jnp.concatenate(heads, axis=1), wxo_ref[...])


def _mix_xattn(x, ya, yb, km, vm, g_out_b, w_out, g_xattn, w_xq, w_xo):
    b, t, _ = x.shape
    tm = TOKEN_TILE
    tok = lambda w: pl.BlockSpec((1, tm, w), lambda bi, ti: (bi, ti, 0))
    mem = pl.BlockSpec((1, N_MEM, D_XA), lambda bi, ti: (bi, 0, 0))
    return pl.pallas_call(
        _mix_xattn_body,
        grid=(b, t // tm),
        in_specs=[tok(D_MODEL), tok(D_SGU), tok(D_NA), mem, mem, _whole((1, D_NA)),
                  _whole(w_out.shape), _whole((1, D_MODEL)), _whole(w_xq.shape), _whole(w_xo.shape)],
        out_specs=tok(D_MODEL),
        out_shape=jax.ShapeDtypeStruct((b, t, D_MODEL), F32),
        compiler_params=_params("parallel", "parallel"),
        name="mix_xattn",
    )(x, ya, yb, km, vm, g_out_b, w_out, g_xattn, w_xq, w_xo)


def _route(logits, grp):
    lane = lax.broadcasted_iota(jnp.int32, logits.shape, 1)
    lg = jnp.where(lane < N_GROUPS, logits, NEG_BIG)
    gmax = jnp.max(lg, axis=-1, keepdims=True)
    l_sel = jnp.sum(jnp.where(lane == grp, logits, 0.0), axis=-1, keepdims=True)
    g_top = jnp.exp(l_sel - gmax) / jnp.sum(jnp.exp(lg - gmax), axis=-1, keepdims=True)
    first = N_GROUPS + grp * EXPERTS_PER_GROUP
    le = jnp.where((lane >= first) & (lane < first + EXPERTS_PER_GROUP), logits, NEG_BIG)
    m1 = jnp.max(le, axis=-1, keepdims=True)
    i1 = jnp.min(jnp.where(le == m1, lane, LANES), axis=-1, keepdims=True)
    le2 = jnp.where(lane == i1, NEG_BIG, le)
    m2 = jnp.max(le2, axis=-1, keepdims=True)
    i2 = jnp.min(jnp.where(le2 == m2, lane, LANES), axis=-1, keepdims=True)
    p2 = jnp.exp(m2 - m1)
    w1 = g_top / (1.0 + p2)
    w2 = g_top * p2 / (1.0 + p2)
    return jnp.where(lane == i1, w1, 0.0) + jnp.where(lane == i2, w2, 0.0)


def _split_bf16(x):
    hi = x.astype(BF16)
    return hi, (x - hi.astype(F32)).astype(BF16)


GROUP_ROWS = 8


def _group_body(h_ref, gm_ref, wrt_ref, gid_ref, cnt_ref):
    hn_hi, hn_lo = _split_bf16(_rms(h_ref[...], gm_ref[...]))
    part = _dot_nt(wrt_ref[...], hn_hi) + _dot_nt(wrt_ref[...], hn_lo)
    logit_t = part[:LANES] + part[LANES:]
    row = lax.broadcasted_iota(jnp.int32, (GROUP_ROWS, logit_t.shape[1]), 0)
    lg = jnp.where(row < N_GROUPS, logit_t[:GROUP_ROWS], NEG_BIG)
    gmax = jnp.max(lg, axis=0, keepdims=True)
    gid = jnp.min(jnp.where(lg == gmax, row, GROUP_ROWS), axis=0, keepdims=True)
    gid_ref[0] = gid

    @pl.when(pl.program_id(0) == 0)
    def _():
        cnt_ref[...] = jnp.zeros_like(cnt_ref)
    cnt_ref[...] += jnp.sum((row == gid).astype(F32), axis=1, keepdims=True)


def _group_of_tokens(h2, g_moe, wrt):
    n = h2.shape[0]
    tm = TOKEN_TILE
    return pl.pallas_call(
        _group_body,
        grid=(n // tm,),
        in_specs=[pl.BlockSpec((tm, D_MODEL), lambda i: (i, 0)), _whole((1, D_MODEL)), _whole(wrt.shape)],
        out_specs=[pl.BlockSpec((1, 1, tm), lambda i: (i, 0, 0)), _whole((GROUP_ROWS, LANES))],
        out_shape=[jax.ShapeDtypeStruct((n // tm, 1, tm), jnp.int32),
                   jax.ShapeDtypeStruct((GROUP_ROWS, LANES), F32)],
        compiler_params=_params("arbitrary"),
        name="moe_group",
    )(h2, g_moe, wrt)


def _slot_body(start_ref, gid_ref, before_ref, pos_ref, run_ref):
    @pl.when(pl.program_id(0) == 0)
    def _():
        run_ref[...] = jnp.zeros_like(run_ref)
    gid = gid_ref[0]
    row = lax.broadcasted_iota(jnp.int32, (GROUP_ROWS, gid.shape[1]), 0)
    onehot = row == gid
    earlier = _dot(onehot.astype(BF16), before_ref[...])
    row1 = lax.broadcasted_iota(jnp.int32, (GROUP_ROWS, 1), 0)
    start = jnp.zeros((GROUP_ROWS, 1), F32)
    for g in range(N_GROUPS):
        start = jnp.where(row1 == g, start_ref[g].astype(F32), start)
    slot = jnp.sum(jnp.where(onehot, earlier + (start + run_ref[:, 0:1]), 0.0), axis=0, keepdims=True)
    pos_ref[0] = slot.astype(jnp.int32)
    run_ref[...] += jnp.sum(onehot.astype(F32), axis=1, keepdims=True)


def _sorted_slots(start, gid, before):
    n_tiles, _, tm = gid.shape
    blk = pl.BlockSpec((1, 1, tm), lambda i, *_: (i, 0, 0))
    return pl.pallas_call(
        _slot_body,
        grid_spec=pltpu.PrefetchScalarGridSpec(
            num_scalar_prefetch=1, grid=(n_tiles,),
            in_specs=[blk, pl.BlockSpec(before.shape, lambda i, *_: (0, 0))], out_specs=blk,
            scratch_shapes=[pltpu.VMEM((GROUP_ROWS, LANES), F32)]),
        out_shape=jax.ShapeDtypeStruct(gid.shape, jnp.int32),
        compiler_params=_params("arbitrary"),
        name="moe_slots",
    )(start, gid, before)


N_FILL = N_GROUPS + 1
FILL_BLOCK = 64
SLAB = 8
assert SLAB * LANES == D_MODEL


def _rows_to_slabs(x, slab_ref):
    tm = x.shape[0]
    for k in range(SLAB):
        slab_ref[pl.ds(k, tm, stride=SLAB), :] = x[:, k * LANES:(k + 1) * LANES]


def _slabs_to_rows(slab_ref, tm):
    return jnp.concatenate([slab_ref[pl.ds(k, tm, stride=SLAB), :] for k in range(SLAB)], axis=1)


def _tile_of(r):
    return pl.ds(pl.multiple_of(r * SLAB, SLAB), SLAB)


TOKENS_PER_CHUNK = SLAB


def _chunk_rows(c):
    return pl.ds(pl.multiple_of(c * TOKENS_PER_CHUNK, TOKENS_PER_CHUNK), TOKENS_PER_CHUNK)


def _chunk_slab_rows(c, k):
    return pl.ds(c * TOKENS_PER_CHUNK * SLAB + k, TOKENS_PER_CHUNK, stride=SLAB)


def _start_chunk_copies(c, copy_of):
    for u in range(TOKENS_PER_CHUNK):
        copy_of(c * TOKENS_PER_CHUNK + u).start(priority=u % 2)


def _scatter_body(fill_ref, pos_ref, h_ref, hs_ref, slab_ref, zero_ref, sem, fill_sem):
    tm = h_ref.shape[0]
    i = pl.program_id(0)
    slot = i % 2

    def wait_slot(sl):
        pltpu.make_async_copy(slab_ref.at[sl], hs_ref.at[pl.ds(0, tm * SLAB)], sem.at[sl]).wait()

    @pl.when(i >= 2)
    def _():
        wait_slot(slot)

    def row_copy(t):
        return pltpu.make_async_copy(slab_ref.at[slot, _tile_of(t)], hs_ref.at[_tile_of(pos_ref[0, 0, t])],
                                     sem.at[slot])

    def chunk(c, carry):
        for k in range(SLAB):
            slab_ref[slot, _chunk_slab_rows(c, k), :] = h_ref[_chunk_rows(c), k * LANES:(k + 1) * LANES]
        _start_chunk_copies(c, row_copy)
        return carry
    lax.fori_loop(0, tm // TOKENS_PER_CHUNK, chunk, 0, unroll=2)

    @pl.when(i == 0)
    def _():
        zero_ref[...] = jnp.zeros_like(zero_ref)
        for f in range(N_FILL):
            first, length = fill_ref[f], fill_ref[N_FILL + f]
            n_blocks = length // FILL_BLOCK

            def block_copy(j, first=first):
                rows = pl.ds(pl.multiple_of((first + j * FILL_BLOCK) * SLAB, SLAB), FILL_BLOCK * SLAB)
                return pltpu.make_async_copy(zero_ref, hs_ref.at[rows], fill_sem)

            def row_copy(j, first=first, n_blocks=n_blocks):
                return pltpu.make_async_copy(zero_ref.at[pl.ds(0, SLAB)],
                                             hs_ref.at[_tile_of(first + n_blocks * FILL_BLOCK + j)], fill_sem)

            for copy_of, count in ((block_copy, n_blocks), (row_copy, length - n_blocks * FILL_BLOCK)):
                def fill_start(j, carry, copy_of=copy_of):
                    copy_of(j).start()
                    return carry

                def fill_wait(j, carry, copy_of=copy_of):
                    copy_of(j).wait()
                    return carry
                lax.fori_loop(0, count, fill_start, 0)
                lax.fori_loop(0, count, fill_wait, 0)

    @pl.when(i == pl.num_programs(0) - 1)
    def _():
        @pl.when(i >= 1)
        def _():
            wait_slot(1 - slot)
        wait_slot(slot)


def _scatter_rows(fill, pos, h2, n_out):
    n_tiles, _, tm = pos.shape
    return pl.pallas_call(
        _scatter_body,
        grid_spec=pltpu.PrefetchScalarGridSpec(
            num_scalar_prefetch=1, grid=(n_tiles,),
            in_specs=[pl.BlockSpec((1, 1, tm), lambda i, *_: (i, 0, 0), memory_space=pltpu.SMEM),
                      pl.BlockSpec((tm, D_MODEL), lambda i, *_: (i, 0))],
            out_specs=pl.BlockSpec(memory_space=pl.ANY),
            scratch_shapes=[pltpu.VMEM((2, tm * SLAB, LANES), F32), pltpu.VMEM((FILL_BLOCK * SLAB, LANES), F32),
                            pltpu.SemaphoreType.DMA((2,)), pltpu.SemaphoreType.DMA]),
        out_shape=jax.ShapeDtypeStruct((n_out * SLAB, LANES), F32),
        compiler_params=_params("arbitrary"),
        name="moe_scatter_rows",
    )(fill, pos, h2)


def _gather_body(pos_ref, next_pos_ref, ys_ref, y_ref, slab_ref, sem):
    tm = y_ref.shape[0]
    i = pl.program_id(0)
    slot = i % 2

    def fetch(p_ref, sl):
        def chunk(c, carry):
            _start_chunk_copies(c, lambda t: pltpu.make_async_copy(
                ys_ref.at[_tile_of(p_ref[0, 0, t])], slab_ref.at[sl, _tile_of(t)], sem.at[sl]))
            return carry
        lax.fori_loop(0, tm // TOKENS_PER_CHUNK, chunk, 0)

    @pl.when(i == 0)
    def _():
        fetch(pos_ref, 0)

    @pl.when(i + 1 < pl.num_programs(0))
    def _():
        fetch(next_pos_ref, 1 - slot)

    pltpu.make_async_copy(ys_ref.at[pl.ds(0, tm * SLAB)], slab_ref.at[slot], sem.at[slot]).wait()
    y_ref[...] = _slabs_to_rows(slab_ref.at[slot], tm)


def _gather_rows(pos, ys, n):
    n_tiles, _, tm = pos.shape
    smem = lambda index_map: pl.BlockSpec((1, 1, tm), index_map, memory_space=pltpu.SMEM)
    return pl.pallas_call(
        _gather_body,
        grid=(n_tiles,),
        in_specs=[smem(lambda i: (i, 0, 0)), smem(lambda i: (jnp.minimum(i + 1, n_tiles - 1), 0, 0)),
                  pl.BlockSpec(memory_space=pl.ANY)],
        out_specs=pl.BlockSpec((tm, D_MODEL), lambda i: (i, 0)),
        out_shape=jax.ShapeDtypeStruct((n, D_MODEL), F32),
        scratch_shapes=[pltpu.VMEM((2, tm * SLAB, LANES), F32), pltpu.SemaphoreType.DMA((2,))],
        compiler_params=_params("arbitrary"),
        name="moe_gather_rows",
    )(pos, pos, ys)


def _moe_body(tile_ref, h_ref, gm_ref, wr_ref, wg_ref, wu_ref, wd_ref, gf_ref, y_ref):
    tm = h_ref.shape[0] // SLAB
    n_used = tile_ref[0]

    @pl.when(pl.program_id(0) >= n_used)
    def _():
        y_ref[...] = jnp.zeros_like(y_ref)

    @pl.when(pl.program_id(0) < n_used)
    def _():
        _expert_tile(tile_ref[1 + pl.program_id(0)], tm, h_ref, gm_ref, wr_ref, wg_ref, wu_ref, wd_ref, gf_ref, y_ref)


def _expert_tile(grp, tm, h_ref, gm_ref, wr_ref, wg_ref, wu_ref, wd_ref, gf_ref, y_ref):
    h = _slabs_to_rows(h_ref, tm)
    hn_hi, hn_lo = _split_bf16(_rms(h, gm_ref[...]))
    part = _dot(hn_hi, wr_ref[...]) + _dot(hn_lo, wr_ref[...])
    gates = _route(part[:, :LANES] + part[:, LANES:], grp)
    lane = lax.broadcasted_iota(jnp.int32, gates.shape, 1)
    first = lax.broadcasted_iota(jnp.int32, (tm, 2 * D_EXPERT), 1) < D_EXPERT
    acts = []
    for j in range(EXPERTS_PER_GROUP // 2):
        e0 = N_GROUPS + grp * EXPERTS_PER_GROUP + 2 * j
        g0 = jnp.sum(jnp.where(lane == e0, gates, 0.0), axis=-1, keepdims=True)
        g1 = jnp.sum(jnp.where(lane == e0 + 1, gates, 0.0), axis=-1, keepdims=True)
        hg = _dot(hn_hi, wg_ref[j])
        hu = _dot(hn_hi, wu_ref[j])
        acts.append((jax.nn.silu(hg) * hu * jnp.where(first, g0, g1)).astype(BF16))
    y = _rms(h + _dot(jnp.concatenate(acts, axis=1), wd_ref[0]), gf_ref[...])
    _rows_to_slabs(y, y_ref)


def _moe_sorted(tiles, hs, g_moe, wr, wg2, wu2, wd2, g_final):
    blk = TOKEN_TILE * SLAB
    n_steps = hs.shape[0] // blk
    ppg = EXPERTS_PER_GROUP // 2
    used = lambda s, tl: jnp.minimum(s, tl[0] - 1)
    grp = lambda s, tl: tl[1 + used(s, tl)]
    const = lambda shape: pl.BlockSpec(shape, lambda s, tl: (0,) * len(shape))
    return pl.pallas_call(
        _moe_body,
        grid_spec=pltpu.PrefetchScalarGridSpec(
            num_scalar_prefetch=1, grid=(n_steps,),
            in_specs=[pl.BlockSpec((blk, LANES), lambda s, tl: (used(s, tl), 0)), const((1, D_MODEL)),
                      const(wr.shape),
                      pl.BlockSpec((ppg, D_MODEL, 2 * D_EXPERT), lambda s, tl: (grp(s, tl), 0, 0)),
                      pl.BlockSpec((ppg, D_MODEL, 2 * D_EXPERT), lambda s, tl: (grp(s, tl), 0, 0)),
                      pl.BlockSpec((1, ppg * 2 * D_EXPERT, D_MODEL), lambda s, tl: (grp(s, tl), 0, 0)),
                      const((1, D_MODEL))],
            out_specs=pl.BlockSpec((blk, LANES), lambda s, tl: (s, 0))),
        out_shape=jax.ShapeDtypeStruct(hs.shape, F32),
        compiler_params=_params("arbitrary"),
        name="moe",
    )(tiles, hs, g_moe, wr, wg2, wu2, wd2, g_final)


def _moe(h2, w):
    n = h2.shape[0]
    tm = TOKEN_TILE
    n_steps = n // tm + N_GROUPS - 1
    gid, cnt = _group_of_tokens(h2, w["g_moe"], w["wrt"])
    cnt = cnt[:N_GROUPS, 0].astype(jnp.int32)
    padded = (cnt + tm - 1) // tm * tm
    end = jnp.cumsum(padded)
    start = end - padded
    tile_group = jnp.sum(jnp.arange(n_steps)[:, None] >= (end // tm)[None, :], axis=1)
    tiles = jnp.concatenate([end[-1:] // tm, jnp.minimum(tile_group, N_GROUPS - 1)]).astype(jnp.int32)
    fill = jnp.concatenate([start + cnt, end[-1:], padded - cnt, n_steps * tm - end[-1:]]).astype(jnp.int32)
    pos = _sorted_slots(start.astype(jnp.int32), gid, w["before"])
    hs = _scatter_rows(fill, pos, h2, n_steps * tm)
    ys = _moe_sorted(tiles, hs, w["g_moe"], w["wr"], w["wg2"], w["wu2"], w["wd2"], w["g_final"])
    return _gather_rows(pos, ys, n)


def _prep_weights(g_mix, w_in, g_sgu_v, sgu_w, sgu_b, na_rpb, g_out_a, g_out_b, w_out, g_xattn, g_mem,
                  w_xq, w_xkv, w_xo, g_moe, w_router_group, w_router_expert, w_exp_gate, w_exp_up,
                  w_exp_down, g_final):
    row = lambda g: g.reshape(1, -1).astype(F32)
    heads = sgu_w.shape[0]
    wcat = jnp.concatenate([sgu_w[0::2], sgu_w[1::2]], axis=2).astype(BF16)
    bexp = jnp.repeat(sgu_b.T, HEAD_DIM, axis=1).astype(F32)
    assert heads * HEAD_DIM == D_SGU
    wr = jnp.zeros((D_MODEL, LANES), F32)
    wr = wr.at[:, :N_GROUPS].set(w_router_group).at[:, N_GROUPS:N_GROUPS + N_EXPERTS].set(w_router_expert)
    wr_hi = wr.astype(BF16)
    wr_split = jnp.concatenate([wr_hi, (wr - wr_hi.astype(F32)).astype(BF16)], axis=1)
    before = jnp.triu(jnp.ones((TOKEN_TILE, TOKEN_TILE), BF16), k=1)
    pair_cols = lambda w: (w.reshape(N_EXPERTS // 2, 2, D_MODEL, D_EXPERT).transpose(0, 2, 1, 3)
                           .reshape(N_EXPERTS // 2, D_MODEL, 2 * D_EXPERT).astype(BF16))
    return dict(
        g_mix=row(g_mix), w_a=w_in[:, :2 * D_SGU].astype(BF16),
        w_qv=jnp.concatenate([w_in[:, 2 * D_SGU:2 * D_SGU + D_NA], w_in[:, 2 * D_SGU + 2 * D_NA:]], axis=1).astype(BF16),
        w_kt=w_in[:, 2 * D_SGU + D_NA:2 * D_SGU + 2 * D_NA].T.astype(BF16),
        g_sgu_v=row(g_sgu_v), wcat=wcat, bexp=bexp,
        bm=_na_bias_table(na_rpb), g_out_a=row(g_out_a), g_out_b=row(g_out_b), w_out=w_out.astype(BF16),
        g_xattn=row(g_xattn), g_mem=row(g_mem), w_xq=w_xq.astype(BF16), w_xkv=w_xkv.astype(BF16),
        w_xo=w_xo.astype(BF16), g_moe=row(g_moe), wr=wr_split, wrt=wr_split.T, before=before,
        wg2=pair_cols(w_exp_gate), wu2=pair_cols(w_exp_up),
        wd2=w_exp_down.reshape(N_GROUPS, EXPERTS_PER_GROUP * D_EXPERT, D_MODEL).astype(BF16),
        g_final=row(g_final))


def _trunk(x, mem, w):
    b, t, _ = x.shape
    km, vm = _mem_kv(mem, w["g_mem"], w["w_xkv"])
    q, kt, v, ya = _in_proj(x.reshape(b * t, D_MODEL), w["g_mix"], w["w_a"], w["w_qv"], w["w_kt"],
                            w["g_sgu_v"], w["wcat"], w["bexp"], w["g_out_a"])
    seq = lambda a: a.reshape(b, t, -1)
    yb = _na(seq(q), kt, seq(v), w["bm"])
    h2 = _mix_xattn(x, seq(ya), yb, km, vm, w["g_out_b"], w["w_out"], w["g_xattn"], w["w_xq"], w["w_xo"])
    y = _moe(h2.reshape(b * t, D_MODEL), w)
    return y.reshape(b, t, D_MODEL)


def kernel(x_prompt, x_sample, mem_prompt, mem_sample, g_mix, w_in, g_sgu_v, sgu_w, sgu_b, na_rpb, g_out_a,
           g_out_b, w_out, g_xattn, g_mem, w_xq, w_xkv, w_xo, g_moe, w_router_group, w_router_expert,
           w_exp_gate, w_exp_up, w_exp_down, g_final):
    assert g_mix.shape[0] == 1
    w = _prep_weights(g_mix[0], w_in[0], g_sgu_v[0], sgu_w[0], sgu_b[0], na_rpb[0], g_out_a[0], g_out_b[0],
                      w_out[0], g_xattn[0], g_mem[0], w_xq[0], w_xkv[0], w_xo[0], g_moe[0],
                      w_router_group[0], w_router_expert[0], w_exp_gate[0], w_exp_up[0], w_exp_down[0],
                      g_final)
    return (_trunk(x_prompt, mem_prompt, w), _trunk(x_sample, mem_sample, w))
```

```python
import functools
import math

import jax
import jax.numpy as jnp
import numpy as np
from jax import lax
from jax.experimental import pallas as pl
from jax.experimental.pallas import tpu as pltpu

F32 = jnp.float32
BF16 = jnp.bfloat16

D_MODEL = 1024
N_MEM = 256
GRID_W = 64
HEAD_DIM = 64
D_SGU = 512
D_NA = 512
CHUNK = 128
NA_KH = 8
NA_KW = 16
XA_HEADS = 4
XA_HEAD_DIM = 128
D_XA = 512
N_GROUPS = 4
EXPERTS_PER_GROUP = 8
N_EXPERTS = 32
D_EXPERT = 128
EPS = 1e-6

LANES = 128
NEG_BIG = -1e30
LOG2_E = math.log2(math.e)
TOKEN_TILE = 1024
NA_ROW_TILE = 8
VMEM_LIMIT_BYTES = 56 * 1024 * 1024


def _rms(x, g):
    return x * lax.rsqrt(jnp.mean(x * x, axis=-1, keepdims=True) + EPS) * g


def _dot(a, b):
    return jnp.dot(a, b, preferred_element_type=F32)


def _dot_nt(a, b):
    return lax.dot_general(a, b, (((1,), (1,)), ((), ())), preferred_element_type=F32)


def _whole(shape):
    nd = len(shape)
    return pl.BlockSpec(shape, lambda *_: (0,) * nd)


def _params(*sem, flags=None):
    return pltpu.CompilerParams(dimension_semantics=sem, vmem_limit_bytes=VMEM_LIMIT_BYTES, flags=flags)


def _mem_kv_body(mem_ref, g_ref, w_ref, k_ref, v_ref):
    memn = _rms(mem_ref[0], g_ref[...]).astype(BF16)
    kv = _dot(memn, w_ref[...])
    k_ref[0] = kv[:, :D_XA].astype(BF16)
    v_ref[0] = kv[:, D_XA:].astype(BF16)


def _mem_kv(mem, g_mem, w_xkv):
    b = mem.shape[0]
    out = jax.ShapeDtypeStruct((b, N_MEM, D_XA), BF16)
    return pl.pallas_call(
        _mem_kv_body,
        grid=(b,),
        in_specs=[pl.BlockSpec((1, N_MEM, D_MODEL), lambda i: (i, 0, 0)),
                  _whole((1, D_MODEL)), _whole((D_MODEL, 2 * D_XA))],
        out_specs=[pl.BlockSpec((1, N_MEM, D_XA), lambda i: (i, 0, 0))] * 2,
        out_shape=[out, out],
        compiler_params=_params("parallel"),
        name="mem_kv",
    )(mem, g_mem, w_xkv)


def _in_proj_body(x_ref, gmix_ref, wa_ref, wqv_ref, wkt_ref, gv_ref, wcat_ref, bexp_ref, goa_ref,
                  q_ref, kt_ref, v_ref, ya_ref):
    tm = x_ref.shape[0]
    hn = _rms(x_ref[...], gmix_ref[...]).astype(BF16)
    a = jax.nn.gelu(_dot(hn, wa_ref[...]))
    u = a[:, :D_SGU]
    vn = _rms(a[:, D_SGU:], gv_ref[...]).astype(BF16)
    lo = lax.broadcasted_iota(jnp.int32, (CHUNK, LANES), 1) < HEAD_DIM
    zero = jnp.zeros((CHUNK, LANES), BF16)
    chunks = []
    for c in range(tm // CHUNK):
        pairs = []
        for p in range(D_SGU // LANES):
            vp = vn[c * CHUNK:(c + 1) * CHUNK, p * LANES:(p + 1) * LANES]
            rhs = jnp.concatenate([jnp.where(lo, vp, zero), jnp.where(lo, zero, vp)], axis=0)
            pairs.append(_dot(wcat_ref[p], rhs))
        chunks.append(jnp.concatenate(pairs, axis=1) + bexp_ref[...])
    ya = u * jnp.concatenate(chunks, axis=0)
    ya_ref[...] = _rms(ya, goa_ref[...]).astype(BF16)
    qv = _dot(hn, wqv_ref[...])
    q_ref[...] = (qv[:, 0:D_NA] * (LOG2_E * HEAD_DIM ** -0.5)).astype(BF16)
    v_ref[...] = qv[:, D_NA:].astype(BF16)
    kt = _dot_nt(wkt_ref[...], hn).astype(BF16)
    for c in range(tm // LANES):
        kt_ref[c] = kt[:, c * LANES:(c + 1) * LANES]


def _in_proj(x2, g_mix, w_a, w_qv, w_kt, g_sgu_v, wcat, bexp, g_out_a):
    n = x2.shape[0]
    tm = TOKEN_TILE
    out = jax.ShapeDtypeStruct((n, D_NA), BF16)
    tok = lambda w: pl.BlockSpec((tm, w), lambda i: (i, 0))
    return pl.pallas_call(
        _in_proj_body,
        grid=(n // tm,),
        in_specs=[tok(D_MODEL), _whole((1, D_MODEL)), _whole(w_a.shape), _whole(w_qv.shape),
                  _whole(w_kt.shape), _whole((1, D_SGU)), _whole(wcat.shape), _whole(bexp.shape),
                  _whole((1, D_SGU))],
        out_specs=[tok(D_NA), pl.BlockSpec((tm // LANES, D_NA, LANES), lambda i: (i, 0, 0)),
                   tok(D_NA), tok(D_NA)],
        out_shape=[out, jax.ShapeDtypeStruct((n // LANES, D_NA, LANES), BF16), out, out],
        compiler_params=_params("parallel"),
        name="in_proj",
    )(x2, g_mix, w_a, w_qv, w_kt, g_sgu_v, wcat, bexp, g_out_a)


ROWS_PER_SLAB = LANES // GRID_W
SLABS_PER_WINDOW = NA_KH // ROWS_PER_SLAB


def _na_body(rows, q_ref, kt_ref, v_ref, bm_ref, o_ref, kk_ref, s_ref, p_ref):
    n_slabs = rows // ROWS_PER_SLAB
    n_tiles = rows // NA_ROW_TILE
    half = LANES // 2

    def swap_halves(x):
        return jnp.concatenate([x[:, half:], x[:, :half]], axis=1)

    first_half = lax.broadcasted_iota(jnp.int32, (LANES, LANES), 1) < half

    def shift(s, swapped):
        swapped_next = swap_halves(kt_ref[s + 1])
        kk_ref[0, s] = kt_ref[s]
        kk_ref[1, s] = jnp.where(first_half, swapped, swapped_next)
        return swapped_next
    lax.fori_loop(0, n_slabs - 1, shift, swap_halves(kt_ref[0]), unroll=8)
    last = kt_ref[n_slabs - 1]
    kk_ref[0, n_slabs - 1] = last
    kk_ref[1, n_slabs - 1] = last

    s_ref[1] = jnp.zeros(s_ref.shape[1:], F32)
    p_ref[0] = jnp.zeros(p_ref.shape[1:], BF16)

    lo = lax.broadcasted_iota(jnp.int32, (GRID_W, LANES), 1) < HEAD_DIM
    zero = jnp.zeros((GRID_W, LANES), BF16)

    def first_key_row(r):
        return jnp.clip(r - NA_KH // 2, 0, rows - NA_KH)

    def tok(r):
        return pl.multiple_of(r * GRID_W, GRID_W)

    def step(k, carry):
        slot_a = k % 2
        slot_b = (k + 1) % 2
        tile_a = jnp.minimum(k, n_tiles - 1)
        tile_b = jnp.clip(k - 1, 0, n_tiles - 1)
        tile_c = jnp.clip(k - 2, 0, n_tiles - 1)
        for i in range(NA_ROW_TILE):
            r = tile_c * NA_ROW_TILE + i
            vw = v_ref[0, pl.ds(tok(first_key_row(r)), NA_KH * GRID_W), :]
            o = _dot(p_ref[slot_a, i], vw)
            o_ref[0, pl.ds(tok(r), GRID_W), :] = jnp.where(lo, o[:GRID_W], o[GRID_W:]).astype(BF16)
        for i in range(NA_ROW_TILE):
            r = tile_b * NA_ROW_TILE + i
            s = s_ref[slot_b, i] + bm_ref[0, r - first_key_row(r)]
            e = jnp.exp2(s - jnp.max(s, axis=-1, keepdims=True))
            p_ref[slot_b, i] = (e / jnp.sum(e, axis=-1, keepdims=True)).astype(BF16)
        for i in range(NA_ROW_TILE):
            r = tile_a * NA_ROW_TILE + i
            rs = first_key_row(r)
            qrow = q_ref[0, pl.ds(tok(r), GRID_W), :]
            q2 = jnp.concatenate([jnp.where(lo, qrow, zero), jnp.where(lo, zero, qrow)], axis=0)
            odd = rs % ROWS_PER_SLAB
            s0 = rs // ROWS_PER_SLAB
            kw = jnp.concatenate([kk_ref[odd, s0 + j] for j in range(SLABS_PER_WINDOW)], axis=1)
            s_ref[slot_a, i] = _dot(q2, kw)
        return carry

    lax.fori_loop(0, n_tiles + 2, step, 0)


def _na(q, kt, v, bm):
    b, t, _ = q.shape
    rows = t // GRID_W
    assert rows >= 2 * NA_KH and rows % NA_ROW_TILE == 0
    n_pairs = D_NA // LANES
    n_slabs = t // LANES
    seq = pl.BlockSpec((1, t, LANES), lambda bi, p: (bi, 0, p))
    stage = (2, NA_ROW_TILE, 2 * GRID_W, NA_KH * GRID_W)
    return pl.pallas_call(
        functools.partial(_na_body, rows),
        grid=(b, n_pairs),
        in_specs=[seq, pl.BlockSpec((n_slabs, LANES, LANES), lambda bi, p: (bi, p, 0)), seq,
                  pl.BlockSpec((1, NA_KH, 2 * GRID_W, NA_KH * GRID_W), lambda bi, p: (p, 0, 0, 0))],
        out_specs=seq,
        out_shape=jax.ShapeDtypeStruct((b, t, D_NA), BF16),
        scratch_shapes=[pltpu.VMEM((2, n_slabs, LANES, LANES), BF16), pltpu.VMEM(stage, F32),
                        pltpu.VMEM(stage, BF16)],
        compiler_params=_params("parallel", "parallel"),
        name="na",
    )(q, kt, v, bm)


def _na_bias_table(rpb):
    c = np.arange(GRID_W)
    wstart = np.clip(c - NA_KW // 2, 0, GRID_W - NA_KW)
    kc = c[None, :]
    inwin = (kc >= wstart[:, None]) & (kc < wstart[:, None] + NA_KW)
    rel = np.clip(kc - c[:, None], -(NA_KW - 1), NA_KW - 1) + NA_KW - 1
    pick = jnp.asarray((rel[None] == np.arange(2 * NA_KW - 1)[:, None, None]).astype(np.float32))
    tab = jnp.einsum("hrk,kcn->hrcn", rpb * LOG2_E, pick, precision=lax.Precision.HIGHEST)
    tab = jnp.where(jnp.asarray(inwin)[None, None], tab, NEG_BIG)
    bmh = jnp.stack([tab[:, NA_KH - 1 - d:2 * NA_KH - 1 - d] for d in range(NA_KH)], axis=1)
    h = rpb.shape[0]
    bmh = bmh.transpose(0, 1, 3, 2, 4).reshape(h // 2, 2, NA_KH, GRID_W, NA_KH * GRID_W)
    return bmh.transpose(0, 2, 1, 3, 4).reshape(h // 2, NA_KH, 2 * GRID_W, NA_KH * GRID_W).astype(F32)


def _mix_xattn_body(x_ref, ya_ref, yb_ref, km_ref, vm_ref, gob_ref, wout_ref, gx_ref, wxq_ref,
                    wxo_ref, h2_ref):
    ybn = _rms(yb_ref[0].astype(F32), gob_ref[...]).astype(BF16)
    h1 = x_ref[0] + _dot(ya_ref[0], wout_ref[0:D_SGU, :]) + _dot(ybn, wout_ref[D_SGU:, :])
    hn = _rms(h1, gx_ref[...]).astype(BF16)
    q = _dot(hn, wxq_ref[...]).astype(BF16)
    heads = []
    for h in range(XA_HEADS):
        sl = slice(h * XA_HEAD_DIM, (h + 1) * XA_HEAD_DIM)
        s = _dot_nt(q[:, sl], km_ref[0, :, sl]) * (XA_HEAD_DIM ** -0.5)
        e = jnp.exp(s - jnp.max(s, axis=-1, keepdims=True))
        o = _dot(e.astype(BF16), vm_ref[0, :, sl]) / jnp.sum(e, axis=-1, keepdims=True)
        heads.append(o.astype(BF16))
    h2_ref[0] = h1 + _dot(jnp.concatenate(heads, axis=1), wxo_ref[...])


def _mix_xattn(x, ya, yb, km, vm, g_out_b, w_out, g_xattn, w_xq, w_xo):
    b, t, _ = x.shape
    tm = TOKEN_TILE
    tok = lambda w: pl.BlockSpec((1, tm, w), lambda bi, ti: (bi, ti, 0))
    mem = pl.BlockSpec((1, N_MEM, D_XA), lambda bi, ti: (bi, 0, 0))
    return pl.pallas_call(
        _mix_xattn_body,
        grid=(b, t // tm),
        in_specs=[tok(D_MODEL), tok(D_SGU), tok(D_NA), mem, mem, _whole((1, D_NA)),
                  _whole(w_out.shape), _whole((1, D_MODEL)), _whole(w_xq.shape), _whole(w_xo.shape)],
        out_specs=tok(D_MODEL),
        out_shape=jax.ShapeDtypeStruct((b, t, D_MODEL), F32),
        compiler_params=_params("parallel", "parallel"),
        name="mix_xattn",
    )(x, ya, yb, km, vm, g_out_b, w_out, g_xattn, w_xq, w_xo)


def _route(logits, grp):
    lane = lax.broadcasted_iota(jnp.int32, logits.shape, 1)
    lg = jnp.where(lane < N_GROUPS, logits, NEG_BIG)
    gmax = jnp.max(lg, axis=-1, keepdims=True)
    l_sel = jnp.sum(jnp.where(lane == grp, logits, 0.0), axis=-1, keepdims=True)
    g_top = jnp.exp(l_sel - gmax) / jnp.sum(jnp.exp(lg - gmax), axis=-1, keepdims=True)
    first = N_GROUPS + grp * EXPERTS_PER_GROUP
    le = jnp.where((lane >= first) & (lane < first + EXPERTS_PER_GROUP), logits, NEG_BIG)
    m1 = jnp.max(le, axis=-1, keepdims=True)
    i1 = jnp.min(jnp.where(le == m1, lane, LANES), axis=-1, keepdims=True)
    le2 = jnp.where(lane == i1, NEG_BIG, le)
    m2 = jnp.max(le2, axis=-1, keepdims=True)
    i2 = jnp.min(jnp.where(le2 == m2, lane, LANES), axis=-1, keepdims=True)
    p2 = jnp.exp(m2 - m1)
    w1 = g_top / (1.0 + p2)
    w2 = g_top * p2 / (1.0 + p2)
    return jnp.where(lane == i1, w1, 0.0) + jnp.where(lane == i2, w2, 0.0)


def _split_bf16(x):
    hi = x.astype(BF16)
    return hi, (x - hi.astype(F32)).astype(BF16)


GROUP_ROWS = 8


def _group_body(h_ref, gm_ref, wrt_ref, gid_ref, cnt_ref):
    hn_hi, hn_lo = _split_bf16(_rms(h_ref[...], gm_ref[...]))
    part = _dot_nt(wrt_ref[...], hn_hi) + _dot_nt(wrt_ref[...], hn_lo)
    logit_t = part[:LANES] + part[LANES:]
    row = lax.broadcasted_iota(jnp.int32, (GROUP_ROWS, logit_t.shape[1]), 0)
    lg = jnp.where(row < N_GROUPS, logit_t[:GROUP_ROWS], NEG_BIG)
    gmax = jnp.max(lg, axis=0, keepdims=True)
    gid = jnp.min(jnp.where(lg == gmax, row, GROUP_ROWS), axis=0, keepdims=True)
    gid_ref[0] = gid

    @pl.when(pl.program_id(0) == 0)
    def _():
        cnt_ref[...] = jnp.zeros_like(cnt_ref)
    cnt_ref[...] += jnp.sum((row == gid).astype(F32), axis=1, keepdims=True)


def _group_of_tokens(h2, g_moe, wrt):
    n = h2.shape[0]
    tm = TOKEN_TILE
    return pl.pallas_call(
        _group_body,
        grid=(n // tm,),
        in_specs=[pl.BlockSpec((tm, D_MODEL), lambda i: (i, 0)), _whole((1, D_MODEL)), _whole(wrt.shape)],
        out_specs=[pl.BlockSpec((1, 1, tm), lambda i: (i, 0, 0)), _whole((GROUP_ROWS, LANES))],
        out_shape=[jax.ShapeDtypeStruct((n // tm, 1, tm), jnp.int32),
                   jax.ShapeDtypeStruct((GROUP_ROWS, LANES), F32)],
        compiler_params=_params("arbitrary"),
        name="moe_group",
    )(h2, g_moe, wrt)


def _slot_body(start_ref, gid_ref, before_ref, pos_ref, run_ref):
    @pl.when(pl.program_id(0) == 0)
    def _():
        run_ref[...] = jnp.zeros_like(run_ref)
    gid = gid_ref[0]
    row = lax.broadcasted_iota(jnp.int32, (GROUP_ROWS, gid.shape[1]), 0)
    onehot = row == gid
    earlier = _dot(onehot.astype(BF16), before_ref[...])
    row1 = lax.broadcasted_iota(jnp.int32, (GROUP_ROWS, 1), 0)
    start = jnp.zeros((GROUP_ROWS, 1), F32)
    for g in range(N_GROUPS):
        start = jnp.where(row1 == g, start_ref[g].astype(F32), start)
    slot = jnp.sum(jnp.where(onehot, earlier + (start + run_ref[:, 0:1]), 0.0), axis=0, keepdims=True)
    pos_ref[0] = slot.astype(jnp.int32)
    run_ref[...] += jnp.sum(onehot.astype(F32), axis=1, keepdims=True)


def _sorted_slots(start, gid, before):
    n_tiles, _, tm = gid.shape
    blk = pl.BlockSpec((1, 1, tm), lambda i, *_: (i, 0, 0))
    return pl.pallas_call(
        _slot_body,
        grid_spec=pltpu.PrefetchScalarGridSpec(
            num_scalar_prefetch=1, grid=(n_tiles,),
            in_specs=[blk, pl.BlockSpec(before.shape, lambda i, *_: (0, 0))], out_specs=blk,
            scratch_shapes=[pltpu.VMEM((GROUP_ROWS, LANES), F32)]),
        out_shape=jax.ShapeDtypeStruct(gid.shape, jnp.int32),
        compiler_params=_params("arbitrary"),
        name="moe_slots",
    )(start, gid, before)


N_FILL = N_GROUPS + 1
FILL_BLOCK = 64
SLAB = 8
assert SLAB * LANES == D_MODEL


def _rows_to_slabs(x, slab_ref):
    tm = x.shape[0]
    for k in range(SLAB):
        slab_ref[pl.ds(k, tm, stride=SLAB), :] = x[:, k * LANES:(k + 1) * LANES]


def _slabs_to_rows(slab_ref, tm):
    return jnp.concatenate([slab_ref[pl.ds(k, tm, stride=SLAB), :] for k in range(SLAB)], axis=1)


def _tile_of(r):
    return pl.ds(pl.multiple_of(r * SLAB, SLAB), SLAB)


TOKENS_PER_CHUNK = SLAB


def _chunk_rows(c):
    return pl.ds(pl.multiple_of(c * TOKENS_PER_CHUNK, TOKENS_PER_CHUNK), TOKENS_PER_CHUNK)


def _chunk_slab_rows(c, k):
    return pl.ds(c * TOKENS_PER_CHUNK * SLAB + k, TOKENS_PER_CHUNK, stride=SLAB)


def _start_chunk_copies(c, copy_of):
    for u in range(TOKENS_PER_CHUNK):
        copy_of(c * TOKENS_PER_CHUNK + u).start(priority=u % 2)


def _scatter_body(fill_ref, pos_ref, h_ref, hs_ref, slab_ref, zero_ref, sem, fill_sem):
    tm = h_ref.shape[0]
    i = pl.program_id(0)
    slot = i % 2

    def wait_slot(sl):
        pltpu.make_async_copy(slab_ref.at[sl], hs_ref.at[pl.ds(0, tm * SLAB)], sem.at[sl]).wait()

    @pl.when(i >= 2)
    def _():
        wait_slot(slot)

    def row_copy(t):
        return pltpu.make_async_copy(slab_ref.at[slot, _tile_of(t)], hs_ref.at[_tile_of(pos_ref[0, 0, t])],
                                     sem.at[slot])

    def chunk(c, carry):
        for k in range(SLAB):
            slab_ref[slot, _chunk_slab_rows(c, k), :] = h_ref[_chunk_rows(c), k * LANES:(k + 1) * LANES]
        _start_chunk_copies(c, row_copy)
        return carry
    lax.fori_loop(0, tm // TOKENS_PER_CHUNK, chunk, 0, unroll=2)

    @pl.when(i == 0)
    def _():
        zero_ref[...] = jnp.zeros_like(zero_ref)
        for f in range(N_FILL):
            first, length = fill_ref[f], fill_ref[N_FILL + f]
            n_blocks = length // FILL_BLOCK

            def block_copy(j, first=first):
                rows = pl.ds(pl.multiple_of((first + j * FILL_BLOCK) * SLAB, SLAB), FILL_BLOCK * SLAB)
                return pltpu.make_async_copy(zero_ref, hs_ref.at[rows], fill_sem)

            def row_copy(j, first=first, n_blocks=n_blocks):
                return pltpu.make_async_copy(zero_ref.at[pl.ds(0, SLAB)],
                                             hs_ref.at[_tile_of(first + n_blocks * FILL_BLOCK + j)], fill_sem)

            for copy_of, count in ((block_copy, n_blocks), (row_copy, length - n_blocks * FILL_BLOCK)):
                def fill_start(j, carry, copy_of=copy_of):
                    copy_of(j).start()
                    return carry

                def fill_wait(j, carry, copy_of=copy_of):
                    copy_of(j).wait()
                    return carry
                lax.fori_loop(0, count, fill_start, 0)
                lax.fori_loop(0, count, fill_wait, 0)

    @pl.when(i == pl.num_programs(0) - 1)
    def _():
        @pl.when(i >= 1)
        def _():
            wait_slot(1 - slot)
        wait_slot(slot)


def _scatter_rows(fill, pos, h2, n_out):
    n_tiles, _, tm = pos.shape
    return pl.pallas_call(
        _scatter_body,
        grid_spec=pltpu.PrefetchScalarGridSpec(
            num_scalar_prefetch=1, grid=(n_tiles,),
            in_specs=[pl.BlockSpec((1, 1, tm), lambda i, *_: (i, 0, 0), memory_space=pltpu.SMEM),
                      pl.BlockSpec((tm, D_MODEL), lambda i, *_: (i, 0))],
            out_specs=pl.BlockSpec(memory_space=pl.ANY),
            scratch_shapes=[pltpu.VMEM((2, tm * SLAB, LANES), F32), pltpu.VMEM((FILL_BLOCK * SLAB, LANES), F32),
                            pltpu.SemaphoreType.DMA((2,)), pltpu.SemaphoreType.DMA]),
        out_shape=jax.ShapeDtypeStruct((n_out * SLAB, LANES), F32),
        compiler_params=_params("arbitrary"),
        name="moe_scatter_rows",
    )(fill, pos, h2)


def _gather_body(pos_ref, next_pos_ref, ys_ref, y_ref, slab_ref, sem):
    tm = y_ref.shape[0]
    i = pl.program_id(0)
    slot = i % 2

    def fetch(p_ref, sl):
        def chunk(c, carry):
            _start_chunk_copies(c, lambda t: pltpu.make_async_copy(
                ys_ref.at[_tile_of(p_ref[0, 0, t])], slab_ref.at[sl, _tile_of(t)], sem.at[sl]))
            return carry
        lax.fori_loop(0, tm // TOKENS_PER_CHUNK, chunk, 0)

    @pl.when(i == 0)
    def _():
        fetch(pos_ref, 0)

    @pl.when(i + 1 < pl.num_programs(0))
    def _():
        fetch(next_pos_ref, 1 - slot)

    pltpu.make_async_copy(ys_ref.at[pl.ds(0, tm * SLAB)], slab_ref.at[slot], sem.at[slot]).wait()
    y_ref[...] = _slabs_to_rows(slab_ref.at[slot], tm)


def _gather_rows(pos, ys, n):
    n_tiles, _, tm = pos.shape
    smem = lambda index_map: pl.BlockSpec((1, 1, tm), index_map, memory_space=pltpu.SMEM)
    return pl.pallas_call(
        _gather_body,
        grid=(n_tiles,),
        in_specs=[smem(lambda i: (i, 0, 0)), smem(lambda i: (jnp.minimum(i + 1, n_tiles - 1), 0, 0)),
                  pl.BlockSpec(memory_space=pl.ANY)],
        out_specs=pl.BlockSpec((tm, D_MODEL), lambda i: (i, 0)),
        out_shape=jax.ShapeDtypeStruct((n, D_MODEL), F32),
        scratch_shapes=[pltpu.VMEM((2, tm * SLAB, LANES), F32), pltpu.SemaphoreType.DMA((2,))],
        compiler_params=_params("arbitrary"),
        name="moe_gather_rows",
    )(pos, pos, ys)


def _moe_body(tile_ref, h_ref, gm_ref, wr_ref, wg_ref, wu_ref, wd_ref, gf_ref, y_ref):
    tm = h_ref.shape[0] // SLAB
    n_used = tile_ref[0]

    @pl.when(pl.program_id(0) >= n_used)
    def _():
        y_ref[...] = jnp.zeros_like(y_ref)

    @pl.when(pl.program_id(0) < n_used)
    def _():
        _expert_tile(tile_ref[1 + pl.program_id(0)], tm, h_ref, gm_ref, wr_ref, wg_ref, wu_ref, wd_ref, gf_ref, y_ref)


def _expert_tile(grp, tm, h_ref, gm_ref, wr_ref, wg_ref, wu_ref, wd_ref, gf_ref, y_ref):
    h = _slabs_to_rows(h_ref, tm)
    hn_hi, hn_lo = _split_bf16(_rms(h, gm_ref[...]))
    part = _dot(hn_hi, wr_ref[...]) + _dot(hn_lo, wr_ref[...])
    gates = _route(part[:, :LANES] + part[:, LANES:], grp)
    lane = lax.broadcasted_iota(jnp.int32, gates.shape, 1)
    first = lax.broadcasted_iota(jnp.int32, (tm, 2 * D_EXPERT), 1) < D_EXPERT
    acts = []
    for j in range(EXPERTS_PER_GROUP // 2):
        e0 = N_GROUPS + grp * EXPERTS_PER_GROUP + 2 * j
        g0 = jnp.sum(jnp.where(lane == e0, gates, 0.0), axis=-1, keepdims=True)
        g1 = jnp.sum(jnp.where(lane == e0 + 1, gates, 0.0), axis=-1, keepdims=True)
        hg = _dot(hn_hi, wg_ref[j])
        hu = _dot(hn_hi, wu_ref[j])
        acts.append((jax.nn.silu(hg) * hu * jnp.where(first, g0, g1)).astype(BF16))
    y = _rms(h + _dot(jnp.concatenate(acts, axis=1), wd_ref[0]), gf_ref[...])
    _rows_to_slabs(y, y_ref)


def _moe_sorted(tiles, hs, g_moe, wr, wg2, wu2, wd2, g_final):
    blk = TOKEN_TILE * SLAB
    n_steps = hs.shape[0] // blk
    ppg = EXPERTS_PER_GROUP // 2
    used = lambda s, tl: jnp.minimum(s, tl[0] - 1)
    grp = lambda s, tl: tl[1 + used(s, tl)]
    const = lambda shape: pl.BlockSpec(shape, lambda s, tl: (0,) * len(shape))
    return pl.pallas_call(
        _moe_body,
        grid_spec=pltpu.PrefetchScalarGridSpec(
            num_scalar_prefetch=1, grid=(n_steps,),
            in_specs=[pl.BlockSpec((blk, LANES), lambda s, tl: (used(s, tl), 0)), const((1, D_MODEL)),
                      const(wr.shape),
                      pl.BlockSpec((ppg, D_MODEL, 2 * D_EXPERT), lambda s, tl: (grp(s, tl), 0, 0)),
                      pl.BlockSpec((ppg, D_MODEL, 2 * D_EXPERT), lambda s, tl: (grp(s, tl), 0, 0)),
                      pl.BlockSpec((1, ppg * 2 * D_EXPERT, D_MODEL), lambda s, tl: (grp(s, tl), 0, 0)),
                      const((1, D_MODEL))],
            out_specs=pl.BlockSpec((blk, LANES), lambda s, tl: (s, 0))),
        out_shape=jax.ShapeDtypeStruct(hs.shape, F32),
        compiler_params=_params("arbitrary"),
        name="moe",
    )(tiles, hs, g_moe, wr, wg2, wu2, wd2, g_final)


def _moe(h2, w):
    n = h2.shape[0]
    tm = TOKEN_TILE
    n_steps = n // tm + N_GROUPS - 1
    gid, cnt = _group_of_tokens(h2, w["g_moe"], w["wrt"])
    cnt = cnt[:N_GROUPS, 0].astype(jnp.int32)
    padded = (cnt + tm - 1) // tm * tm
    end = jnp.cumsum(padded)
    start = end - padded
    tile_group = jnp.sum(jnp.arange(n_steps)[:, None] >= (end // tm)[None, :], axis=1)
    tiles = jnp.concatenate([end[-1:] // tm, jnp.minimum(tile_group, N_GROUPS - 1)]).astype(jnp.int32)
    fill = jnp.concatenate([start + cnt, end[-1:], padded - cnt, n_steps * tm - end[-1:]]).astype(jnp.int32)
    pos = _sorted_slots(start.astype(jnp.int32), gid, w["before"])
    hs = _scatter_rows(fill, pos, h2, n_steps * tm)
    ys = _moe_sorted(tiles, hs, w["g_moe"], w["wr"], w["wg2"], w["wu2"], w["wd2"], w["g_final"])
    return _gather_rows(pos, ys, n)


def _prep_weights(g_mix, w_in, g_sgu_v, sgu_w, sgu_b, na_rpb, g_out_a, g_out_b, w_out, g_xattn, g_mem,
                  w_xq, w_xkv, w_xo, g_moe, w_router_group, w_router_expert, w_exp_gate, w_exp_up,
                  w_exp_down, g_final):
    row = lambda g: g.reshape(1, -1).astype(F32)
    heads = sgu_w.shape[0]
    wcat = jnp.concatenate([sgu_w[0::2], sgu_w[1::2]], axis=2).astype(BF16)
    bexp = jnp.repeat(sgu_b.T, HEAD_DIM, axis=1).astype(F32)
    assert heads * HEAD_DIM == D_SGU
    wr = jnp.zeros((D_MODEL, LANES), F32)
    wr = wr.at[:, :N_GROUPS].set(w_router_group).at[:, N_GROUPS:N_GROUPS + N_EXPERTS].set(w_router_expert)
    wr_hi = wr.astype(BF16)
    wr_split = jnp.concatenate([wr_hi, (wr - wr_hi.astype(F32)).astype(BF16)], axis=1)
    before = jnp.triu(jnp.ones((TOKEN_TILE, TOKEN_TILE), BF16), k=1)
    pair_cols = lambda w: (w.reshape(N_EXPERTS // 2, 2, D_MODEL, D_EXPERT).transpose(0, 2, 1, 3)
                           .reshape(N_EXPERTS // 2, D_MODEL, 2 * D_EXPERT).astype(BF16))
    return dict(
        g_mix=row(g_mix), w_a=w_in[:, :2 * D_SGU].astype(BF16),
        w_qv=jnp.concatenate([w_in[:, 2 * D_SGU:2 * D_SGU + D_NA], w_in[:, 2 * D_SGU + 2 * D_NA:]], axis=1).astype(BF16),
        w_kt=w_in[:, 2 * D_SGU + D_NA:2 * D_SGU + 2 * D_NA].T.astype(BF16),
        g_sgu_v=row(g_sgu_v), wcat=wcat, bexp=bexp,
        bm=_na_bias_table(na_rpb), g_out_a=row(g_out_a), g_out_b=row(g_out_b), w_out=w_out.astype(BF16),
        g_xattn=row(g_xattn), g_mem=row(g_mem), w_xq=w_xq.astype(BF16), w_xkv=w_xkv.astype(BF16),
        w_xo=w_xo.astype(BF16), g_moe=row(g_moe), wr=wr_split, wrt=wr_split.T, before=before,
        wg2=pair_cols(w_exp_gate), wu2=pair_cols(w_exp_up),
        wd2=w_exp_down.reshape(N_GROUPS, EXPERTS_PER_GROUP * D_EXPERT, D_MODEL).astype(BF16),
        g_final=row(g_final))


def _trunk(x, mem, w):
    b, t, _ = x.shape
    km, vm = _mem_kv(mem, w["g_mem"], w["w_xkv"])
    q, kt, v, ya = _in_proj(x.reshape(b * t, D_MODEL), w["g_mix"], w["w_a"], w["w_qv"], w["w_kt"],
                            w["g_sgu_v"], w["wcat"], w["bexp"], w["g_out_a"])
    seq = lambda a: a.reshape(b, t, -1)
    yb = _na(seq(q), kt, seq(v), w["bm"])
    h2 = _mix_xattn(x, seq(ya), yb, km, vm, w["g_out_b"], w["w_out"], w["g_xattn"], w["w_xq"], w["w_xo"])
    y = _moe(h2.reshape(b * t, D_MODEL), w)
    return y.reshape(b, t, D_MODEL)


def kernel(x_prompt, x_sample, mem_prompt, mem_sample, g_mix, w_in, g_sgu_v, sgu_w, sgu_b, na_rpb, g_out_a,
           g_out_b, w_out, g_xattn, g_mem, w_xq, w_xkv, w_xo, g_moe, w_router_group, w_router_expert,
           w_exp_gate, w_exp_up, w_exp_down, g_final):
    assert g_mix.shape[0] == 1
    w = _prep_weights(g_mix[0], w_in[0], g_sgu_v[0], sgu_w[0], sgu_b[0], na_rpb[0], g_out_a[0], g_out_b[0],
                      w_out[0], g_xattn[0], g_mem[0], w_xq[0], w_xkv[0], w_xo[0], g_moe[0],
                      w_router_group[0], w_router_expert[0], w_exp_gate[0], w_exp_up[0], w_exp_down[0],
                      g_final)
    return (_trunk(x_prompt, mem_prompt, w), _trunk(x_sample, mem_sample, w))
```

```python
import functools
import math

import jax
import jax.numpy as jnp
import numpy as np
from jax import lax
from jax.experimental import pallas as pl
from jax.experimental.pallas import tpu as pltpu

F32 = jnp.float32
BF16 = jnp.bfloat16

D_MODEL = 1024
N_MEM = 256
GRID_W = 64
HEAD_DIM = 64
D_SGU = 512
D_NA = 512
CHUNK = 128
NA_KH = 8
NA_KW = 16
XA_HEADS = 4
XA_HEAD_DIM = 128
D_XA = 512
N_GROUPS = 4
EXPERTS_PER_GROUP = 8
N_EXPERTS = 32
D_EXPERT = 128
EPS = 1e-6

LANES = 128
NEG_BIG = -1e30
LOG2_E = math.log2(math.e)
TOKEN_TILE = 1024
NA_ROW_TILE = 8
VMEM_LIMIT_BYTES = 56 * 1024 * 1024


def _rms(x, g):
    return x * lax.rsqrt(jnp.mean(x * x, axis=-1, keepdims=True) + EPS) * g


def _dot(a, b):
    return jnp.dot(a, b, preferred_element_type=F32)


def _dot_nt(a, b):
    return lax.dot_general(a, b, (((1,), (1,)), ((), ())), preferred_element_type=F32)


def _whole(shape):
    nd = len(shape)
    return pl.BlockSpec(shape, lambda *_: (0,) * nd)


def _params(*sem, flags=None):
    return pltpu.CompilerParams(dimension_semantics=sem, vmem_limit_bytes=VMEM_LIMIT_BYTES, flags=flags)


def _mem_kv_body(mem_ref, g_ref, w_ref, k_ref, v_ref):
    memn = _rms(mem_ref[0], g_ref[...]).astype(BF16)
    kv = _dot(memn, w_ref[...])
    k_ref[0] = kv[:, :D_XA].astype(BF16)
    v_ref[0] = kv[:, D_XA:].astype(BF16)


def _mem_kv(mem, g_mem, w_xkv):
    b = mem.shape[0]
    out = jax.ShapeDtypeStruct((b, N_MEM, D_XA), BF16)
    return pl.pallas_call(
        _mem_kv_body,
        grid=(b,),
        in_specs=[pl.BlockSpec((1, N_MEM, D_MODEL), lambda i: (i, 0, 0)),
                  _whole((1, D_MODEL)), _whole((D_MODEL, 2 * D_XA))],
        out_specs=[pl.BlockSpec((1, N_MEM, D_XA), lambda i: (i, 0, 0))] * 2,
        out_shape=[out, out],
        compiler_params=_params("parallel"),
        name="mem_kv",
    )(mem, g_mem, w_xkv)


def _in_proj_body(x_ref, gmix_ref, wa_ref, wqv_ref, wkt_ref, gv_ref, wcat_ref, bexp_ref, goa_ref,
                  q_ref, kt_ref, v_ref, ya_ref):
    tm = x_ref.shape[0]
    hn = _rms(x_ref[...], gmix_ref[...]).astype(BF16)
    a = jax.nn.gelu(_dot(hn, wa_ref[...]))
    u = a[:, :D_SGU]
    vn = _rms(a[:, D_SGU:], gv_ref[...]).astype(BF16)
    lo = lax.broadcasted_iota(jnp.int32, (CHUNK, LANES), 1) < HEAD_DIM
    zero = jnp.zeros((CHUNK, LANES), BF16)
    chunks = []
    for c in range(tm // CHUNK):
        pairs = []
        for p in range(D_SGU // LANES):
            vp = vn[c * CHUNK:(c + 1) * CHUNK, p * LANES:(p + 1) * LANES]
            rhs = jnp.concatenate([jnp.where(lo, vp, zero), jnp.where(lo, zero, vp)], axis=0)
            pairs.append(_dot(wcat_ref[p], rhs))
        chunks.append(jnp.concatenate(pairs, axis=1) + bexp_ref[...])
    ya = u * jnp.concatenate(chunks, axis=0)
    ya_ref[...] = _rms(ya, goa_ref[...]).astype(BF16)
    qv = _dot(hn, wqv_ref[...])
    q_ref[...] = (qv[:, 0:D_NA] * (LOG2_E * HEAD_DIM ** -0.5)).astype(BF16)
    v_ref[...] = qv[:, D_NA:].astype(BF16)
    kt = _dot_nt(wkt_ref[...], hn).astype(BF16)
    for c in range(tm // LANES):
        kt_ref[c] = kt[:, c * LANES:(c + 1) * LANES]


def _in_proj(x2, g_mix, w_a, w_qv, w_kt, g_sgu_v, wcat, bexp, g_out_a):
    n = x2.shape[0]
    tm = TOKEN_TILE
    out = jax.ShapeDtypeStruct((n, D_NA), BF16)
    tok = lambda w: pl.BlockSpec((tm, w), lambda i: (i, 0))
    return pl.pallas_call(
        _in_proj_body,
        grid=(n // tm,),
        in_specs=[tok(D_MODEL), _whole((1, D_MODEL)), _whole(w_a.shape), _whole(w_qv.shape),
                  _whole(w_kt.shape), _whole((1, D_SGU)), _whole(wcat.shape), _whole(bexp.shape),
                  _whole((1, D_SGU))],
        out_specs=[tok(D_NA), pl.BlockSpec((tm // LANES, D_NA, LANES), lambda i: (i, 0, 0)),
                   tok(D_NA), tok(D_NA)],
        out_shape=[out, jax.ShapeDtypeStruct((n // LANES, D_NA, LANES), BF16), out, out],
        compiler_params=_params("parallel"),
        name="in_proj",
    )(x2, g_mix, w_a, w_qv, w_kt, g_sgu_v, wcat, bexp, g_out_a)


ROWS_PER_SLAB = LANES // GRID_W
SLABS_PER_WINDOW = NA_KH // ROWS_PER_SLAB


def _na_body(rows, q_ref, kt_ref, v_ref, bm_ref, o_ref, kk_ref, s_ref, p_ref):
    n_slabs = rows // ROWS_PER_SLAB
    n_tiles = rows // NA_ROW_TILE
    half = LANES // 2

    def swap_halves(x):
        return jnp.concatenate([x[:, half:], x[:, :half]], axis=1)

    first_half = lax.broadcasted_iota(jnp.int32, (LANES, LANES), 1) < half

    def shift(s, swapped):
        swapped_next = swap_halves(kt_ref[s + 1])
        kk_ref[0, s] = kt_ref[s]
        kk_ref[1, s] = jnp.where(first_half, swapped, swapped_next)
        return swapped_next
    lax.fori_loop(0, n_slabs - 1, shift, swap_halves(kt_ref[0]), unroll=8)
    last = kt_ref[n_slabs - 1]
    kk_ref[0, n_slabs - 1] = last
    kk_ref[1, n_slabs - 1] = last

    s_ref[1] = jnp.zeros(s_ref.shape[1:], F32)
    p_ref[0] = jnp.zeros(p_ref.shape[1:], BF16)

    lo = lax.broadcasted_iota(jnp.int32, (GRID_W, LANES), 1) < HEAD_DIM
    zero = jnp.zeros((GRID_W, LANES), BF16)

    def first_key_row(r):
        return jnp.clip(r - NA_KH // 2, 0, rows - NA_KH)

    def tok(r):
        return pl.multiple_of(r * GRID_W, GRID_W)

    def step(k, carry):
        slot_a = k % 2
        slot_b = (k + 1) % 2
        tile_a = jnp.minimum(k, n_tiles - 1)
        tile_b = jnp.clip(k - 1, 0, n_tiles - 1)
        tile_c = jnp.clip(k - 2, 0, n_tiles - 1)
        for i in range(NA_ROW_TILE):
            r = tile_c * NA_ROW_TILE + i
            vw = v_ref[0, pl.ds(tok(first_key_row(r)), NA_KH * GRID_W), :]
            o = _dot(p_ref[slot_a, i], vw)
            o_ref[0, pl.ds(tok(r), GRID_W), :] = jnp.where(lo, o[:GRID_W], o[GRID_W:]).astype(BF16)
        for i in range(NA_ROW_TILE):
            r = tile_b * NA_ROW_TILE + i
            s = s_ref[slot_b, i] + bm_ref[0, r - first_key_row(r)]
            e = jnp.exp2(s - jnp.max(s, axis=-1, keepdims=True))
            p_ref[slot_b, i] = (e / jnp.sum(e, axis=-1, keepdims=True)).astype(BF16)
        for i in range(NA_ROW_TILE):
            r = tile_a * NA_ROW_TILE + i
            rs = first_key_row(r)
            qrow = q_ref[0, pl.ds(tok(r), GRID_W), :]
            q2 = jnp.concatenate([jnp.where(lo, qrow, zero), jnp.where(lo, zero, qrow)], axis=0)
            odd = rs % ROWS_PER_SLAB
            s0 = rs // ROWS_PER_SLAB
            kw = jnp.concatenate([kk_ref[odd, s0 + j] for j in range(SLABS_PER_WINDOW)], axis=1)
            s_ref[slot_a, i] = _dot(q2, kw)
        return carry

    lax.fori_loop(0, n_tiles + 2, step, 0)


def _na(q, kt, v, bm):
    b, t, _ = q.shape
    rows = t // GRID_W
    assert rows >= 2 * NA_KH and rows % NA_ROW_TILE == 0
    n_pairs = D_NA // LANES
    n_slabs = t // LANES
    seq = pl.BlockSpec((1, t, LANES), lambda bi, p: (bi, 0, p))
    stage = (2, NA_ROW_TILE, 2 * GRID_W, NA_KH * GRID_W)
    return pl.pallas_call(
        functools.partial(_na_body, rows),
        grid=(b, n_pairs),
        in_specs=[seq, pl.BlockSpec((n_slabs, LANES, LANES), lambda bi, p: (bi, p, 0)), seq,
                  pl.BlockSpec((1, NA_KH, 2 * GRID_W, NA_KH * GRID_W), lambda bi, p: (p, 0, 0, 0))],
        out_specs=seq,
        out_shape=jax.ShapeDtypeStruct((b, t, D_NA), BF16),
        scratch_shapes=[pltpu.VMEM((2, n_slabs, LANES, LANES), BF16), pltpu.VMEM(stage, F32),
                        pltpu.VMEM(stage, BF16)],
        compiler_params=_params("parallel", "parallel"),
        name="na",
    )(q, kt, v, bm)


def _na_bias_table(rpb):
    c = np.arange(GRID_W)
    wstart = np.clip(c - NA_KW // 2, 0, GRID_W - NA_KW)
    kc = c[None, :]
    inwin = (kc >= wstart[:, None]) & (kc < wstart[:, None] + NA_KW)
    rel = np.clip(kc - c[:, None], -(NA_KW - 1), NA_KW - 1) + NA_KW - 1
    pick = jnp.asarray((rel[None] == np.arange(2 * NA_KW - 1)[:, None, None]).astype(np.float32))
    tab = jnp.einsum("hrk,kcn->hrcn", rpb * LOG2_E, pick, precision=lax.Precision.HIGHEST)
    tab = jnp.where(jnp.asarray(inwin)[None, None], tab, NEG_BIG)
    bmh = jnp.stack([tab[:, NA_KH - 1 - d:2 * NA_KH - 1 - d] for d in range(NA_KH)], axis=1)
    h = rpb.shape[0]
    bmh = bmh.transpose(0, 1, 3, 2, 4).reshape(h // 2, 2, NA_KH, GRID_W, NA_KH * GRID_W)
    return bmh.transpose(0, 2, 1, 3, 4).reshape(h // 2, NA_KH, 2 * GRID_W, NA_KH * GRID_W).astype(F32)


def _mix_xattn_body(x_ref, ya_ref, yb_ref, km_ref, vm_ref, gob_ref, wout_ref, gx_ref, wxq_ref,
                    wxo_ref, gm_ref, wrt_ref, h2_ref, gid_ref, cnt_ref):
    ybn = _rms(yb_ref[0].astype(F32), gob_ref[...]).astype(BF16)
    h1 = x_ref[0] + _dot(ya_ref[0], wout_ref[0:D_SGU, :]) + _dot(ybn, wout_ref[D_SGU:, :])
    hn = _rms(h1, gx_ref[...]).astype(BF16)
    q = _dot(hn, wxq_ref[...]).astype(BF16)
    heads = []
    for h in range(XA_HEADS):
        sl = slice(h * XA_HEAD_DIM, (h + 1) * XA_HEAD_DIM)
        s = _dot_nt(q[:, sl], km_ref[0, :, sl]) * (XA_HEAD_DIM ** -0.5)
        e = jnp.exp(s - jnp.max(s, axis=-1, keepdims=True))
        o = _dot(e.astype(BF16), vm_ref[0, :, sl]) / jnp.sum(e, axis=-1, keepdims=True)
        heads.append(o.astype(BF16))
    h2 = h1 + _dot(jnp.concatenate(heads, axis=1), wxo_ref[...])
    h2_ref[0] = h2
    first_step = (pl.program_id(0) == 0) & (pl.program_id(1) == 0)
    _group_ids(h2, gm_ref, wrt_ref, gid_ref, cnt_ref, first_step)


def _mix_xattn(x, ya, yb, km, vm, g_out_b, w_out, g_xattn, w_xq, w_xo, g_moe, wrt):
    b, t, _ = x.shape
    tm = TOKEN_TILE
    tiles = t // tm
    tok = lambda w: pl.BlockSpec((1, tm, w), lambda bi, ti: (bi, ti, 0))
    mem = pl.BlockSpec((1, N_MEM, D_XA), lambda bi, ti: (bi, 0, 0))
    return pl.pallas_call(
        _mix_xattn_body,
        grid=(b, tiles),
        in_specs=[tok(D_MODEL), tok(D_SGU), tok(D_NA), mem, mem, _whole((1, D_NA)),
                  _whole(w_out.shape), _whole((1, D_MODEL)), _whole(w_xq.shape), _whole(w_xo.shape),
                  _whole((1, D_MODEL)), _whole(wrt.shape)],
        out_specs=[tok(D_MODEL), pl.BlockSpec((1, 1, tm), lambda bi, ti: (bi * tiles + ti, 0, 0)),
                   _whole((GROUP_ROWS, LANES))],
        out_shape=[jax.ShapeDtypeStruct((b, t, D_MODEL), F32),
                   jax.ShapeDtypeStruct((b * tiles, 1, tm), jnp.int32),
                   jax.ShapeDtypeStruct((GROUP_ROWS, LANES), F32)],
        compiler_params=_params("arbitrary", "arbitrary"),
        name="mix_xattn",
    )(x, ya, yb, km, vm, g_out_b, w_out, g_xattn, w_xq, w_xo, g_moe, wrt)


def _route(logits, grp):
    lane = lax.broadcasted_iota(jnp.int32, logits.shape, 1)
    lg = jnp.where(lane < N_GROUPS, logits, NEG_BIG)
    gmax = jnp.max(lg, axis=-1, keepdims=True)
    l_sel = jnp.sum(jnp.where(lane == grp, logits, 0.0), axis=-1, keepdims=True)
    g_top = jnp.exp(l_sel - gmax) / jnp.sum(jnp.exp(lg - gmax), axis=-1, keepdims=True)
    first = N_GROUPS + grp * EXPERTS_PER_GROUP
    le = jnp.where((lane >= first) & (lane < first + EXPERTS_PER_GROUP), logits, NEG_BIG)
    m1 = jnp.max(le, axis=-1, keepdims=True)
    i1 = jnp.min(jnp.where(le == m1, lane, LANES), axis=-1, keepdims=True)
    le2 = jnp.where(lane == i1, NEG_BIG, le)
    m2 = jnp.max(le2, axis=-1, keepdims=True)
    i2 = jnp.min(jnp.where(le2 == m2, lane, LANES), axis=-1, keepdims=True)
    p2 = jnp.exp(m2 - m1)
    w1 = g_top / (1.0 + p2)
    w2 = g_top * p2 / (1.0 + p2)
    return jnp.where(lane == i1, w1, 0.0) + jnp.where(lane == i2, w2, 0.0)


def _split_bf16(x):
    hi = x.astype(BF16)
    return hi, (x - hi.astype(F32)).astype(BF16)


GROUP_ROWS = 8


def _group_ids(h, gm_ref, wrt_ref, gid_ref, cnt_ref, first_step):
    hn_hi, hn_lo = _split_bf16(_rms(h, gm_ref[...]))
    part = _dot_nt(wrt_ref[...], hn_hi) + _dot_nt(wrt_ref[...], hn_lo)
    logit_t = part[:LANES] + part[LANES:]
    row = lax.broadcasted_iota(jnp.int32, (GROUP_ROWS, logit_t.shape[1]), 0)
    lg = jnp.where(row < N_GROUPS, logit_t[:GROUP_ROWS], NEG_BIG)
    gmax = jnp.max(lg, axis=0, keepdims=True)
    gid = jnp.min(jnp.where(lg == gmax, row, GROUP_ROWS), axis=0, keepdims=True)
    gid_ref[0] = gid

    @pl.when(first_step)
    def _():
        cnt_ref[...] = jnp.zeros_like(cnt_ref)
    cnt_ref[...] += jnp.sum((row == gid).astype(F32), axis=1, keepdims=True)


def _slot_body(start_ref, gid_ref, before_ref, pos_ref, run_ref):
    @pl.when(pl.program_id(0) == 0)
    def _():
        run_ref[...] = jnp.zeros_like(run_ref)
    gid = gid_ref[0]
    row = lax.broadcasted_iota(jnp.int32, (GROUP_ROWS, gid.shape[1]), 0)
    onehot = row == gid
    earlier = _dot(onehot.astype(BF16), before_ref[...])
    row1 = lax.broadcasted_iota(jnp.int32, (GROUP_ROWS, 1), 0)
    start = jnp.zeros((GROUP_ROWS, 1), F32)
    for g in range(N_GROUPS):
        start = jnp.where(row1 == g, start_ref[g].astype(F32), start)
    slot = jnp.sum(jnp.where(onehot, earlier + (start + run_ref[:, 0:1]), 0.0), axis=0, keepdims=True)
    pos_ref[0] = slot.astype(jnp.int32)
    run_ref[...] += jnp.sum(onehot.astype(F32), axis=1, keepdims=True)


def _sorted_slots(start, gid, before):
    n_tiles, _, tm = gid.shape
    blk = pl.BlockSpec((1, 1, tm), lambda i, *_: (i, 0, 0))
    return pl.pallas_call(
        _slot_body,
        grid_spec=pltpu.PrefetchScalarGridSpec(
            num_scalar_prefetch=1, grid=(n_tiles,),
            in_specs=[blk, pl.BlockSpec(before.shape, lambda i, *_: (0, 0))], out_specs=blk,
            scratch_shapes=[pltpu.VMEM((GROUP_ROWS, LANES), F32)]),
        out_shape=jax.ShapeDtypeStruct(gid.shape, jnp.int32),
        compiler_params=_params("arbitrary"),
        name="moe_slots",
    )(start, gid, before)


N_FILL = N_GROUPS + 1
FILL_BLOCK = 64
SLAB = 8
assert SLAB * LANES == D_MODEL


def _rows_to_slabs(x, slab_ref):
    tm = x.shape[0]
    for k in range(SLAB):
        slab_ref[pl.ds(k, tm, stride=SLAB), :] = x[:, k * LANES:(k + 1) * LANES]


def _slabs_to_rows(slab_ref, tm):
    return jnp.concatenate([slab_ref[pl.ds(k, tm, stride=SLAB), :] for k in range(SLAB)], axis=1)


def _tile_of(r):
    return pl.ds(pl.multiple_of(r * SLAB, SLAB), SLAB)


TOKENS_PER_CHUNK = SLAB


def _chunk_rows(c):
    return pl.ds(pl.multiple_of(c * TOKENS_PER_CHUNK, TOKENS_PER_CHUNK), TOKENS_PER_CHUNK)


def _chunk_slab_rows(c, k):
    return pl.ds(c * TOKENS_PER_CHUNK * SLAB + k, TOKENS_PER_CHUNK, stride=SLAB)


def _start_chunk_copies(c, copy_of):
    for u in range(TOKENS_PER_CHUNK):
        copy_of(c * TOKENS_PER_CHUNK + u).start(priority=u % 2)


def _scatter_body(fill_ref, pos_ref, h_ref, hs_ref, slab_ref, zero_ref, sem, fill_sem):
    tm = h_ref.shape[0]
    i = pl.program_id(0)
    slot = i % 2

    def wait_slot(sl):
        pltpu.make_async_copy(slab_ref.at[sl], hs_ref.at[pl.ds(0, tm * SLAB)], sem.at[sl]).wait()

    @pl.when(i >= 2)
    def _():
        wait_slot(slot)

    def row_copy(t):
        return pltpu.make_async_copy(slab_ref.at[slot, _tile_of(t)], hs_ref.at[_tile_of(pos_ref[0, 0, t])],
                                     sem.at[slot])

    def chunk(c, carry):
        for k in range(SLAB):
            slab_ref[slot, _chunk_slab_rows(c, k), :] = h_ref[_chunk_rows(c), k * LANES:(k + 1) * LANES]
        _start_chunk_copies(c, row_copy)
        return carry
    lax.fori_loop(0, tm // TOKENS_PER_CHUNK, chunk, 0, unroll=2)

    @pl.when(i == 0)
    def _():
        zero_ref[...] = jnp.zeros_like(zero_ref)
        for f in range(N_FILL):
            first, length = fill_ref[f], fill_ref[N_FILL + f]
            n_blocks = length // FILL_BLOCK

            def block_copy(j, first=first):
                rows = pl.ds(pl.multiple_of((first + j * FILL_BLOCK) * SLAB, SLAB), FILL_BLOCK * SLAB)
                return pltpu.make_async_copy(zero_ref, hs_ref.at[rows], fill_sem)

            def row_copy(j, first=first, n_blocks=n_blocks):
                return pltpu.make_async_copy(zero_ref.at[pl.ds(0, SLAB)],
                                             hs_ref.at[_tile_of(first + n_blocks * FILL_BLOCK + j)], fill_sem)

            for copy_of, count in ((block_copy, n_blocks), (row_copy, length - n_blocks * FILL_BLOCK)):
                def fill_start(j, carry, copy_of=copy_of):
                    copy_of(j).start()
                    return carry

                def fill_wait(j, carry, copy_of=copy_of):
                    copy_of(j).wait()
                    return carry
                lax.fori_loop(0, count, fill_start, 0)
                lax.fori_loop(0, count, fill_wait, 0)

    @pl.when(i == pl.num_programs(0) - 1)
    def _():
        @pl.when(i >= 1)
        def _():
            wait_slot(1 - slot)
        wait_slot(slot)


def _scatter_rows(fill, pos, h2, n_out):
    n_tiles, _, tm = pos.shape
    return pl.pallas_call(
        _scatter_body,
        grid_spec=pltpu.PrefetchScalarGridSpec(
            num_scalar_prefetch=1, grid=(n_tiles,),
            in_specs=[pl.BlockSpec((1, 1, tm), lambda i, *_: (i, 0, 0), memory_space=pltpu.SMEM),
                      pl.BlockSpec((tm, D_MODEL), lambda i, *_: (i, 0))],
            out_specs=pl.BlockSpec(memory_space=pl.ANY),
            scratch_shapes=[pltpu.VMEM((2, tm * SLAB, LANES), F32), pltpu.VMEM((FILL_BLOCK * SLAB, LANES), F32),
                            pltpu.SemaphoreType.DMA((2,)), pltpu.SemaphoreType.DMA]),
        out_shape=jax.ShapeDtypeStruct((n_out * SLAB, LANES), F32),
        compiler_params=_params("arbitrary"),
        name="moe_scatter_rows",
    )(fill, pos, h2)


def _gather_body(pos_ref, next_pos_ref, ys_ref, y_ref, slab_ref, sem):
    tm = y_ref.shape[0]
    i = pl.program_id(0)
    slot = i % 2

    def fetch(p_ref, sl):
        def chunk(c, carry):
            _start_chunk_copies(c, lambda t: pltpu.make_async_copy(
                ys_ref.at[_tile_of(p_ref[0, 0, t])], slab_ref.at[sl, _tile_of(t)], sem.at[sl]))
            return carry
        lax.fori_loop(0, tm // TOKENS_PER_CHUNK, chunk, 0)

    @pl.when(i == 0)
    def _():
        fetch(pos_ref, 0)

    @pl.when(i + 1 < pl.num_programs(0))
    def _():
        fetch(next_pos_ref, 1 - slot)

    pltpu.make_async_copy(ys_ref.at[pl.ds(0, tm * SLAB)], slab_ref.at[slot], sem.at[slot]).wait()
    y_ref[...] = _slabs_to_rows(slab_ref.at[slot], tm)


def _gather_rows(pos, ys, n):
    n_tiles, _, tm = pos.shape
    smem = lambda index_map: pl.BlockSpec((1, 1, tm), index_map, memory_space=pltpu.SMEM)
    return pl.pallas_call(
        _gather_body,
        grid=(n_tiles,),
        in_specs=[smem(lambda i: (i, 0, 0)), smem(lambda i: (jnp.minimum(i + 1, n_tiles - 1), 0, 0)),
                  pl.BlockSpec(memory_space=pl.ANY)],
        out_specs=pl.BlockSpec((tm, D_MODEL), lambda i: (i, 0)),
        out_shape=jax.ShapeDtypeStruct((n, D_MODEL), F32),
        scratch_shapes=[pltpu.VMEM((2, tm * SLAB, LANES), F32), pltpu.SemaphoreType.DMA((2,))],
        compiler_params=_params("arbitrary"),
        name="moe_gather_rows",
    )(pos, pos, ys)


def _moe_body(tile_ref, h_ref, gm_ref, wr_ref, wg_ref, wu_ref, wd_ref, gf_ref, y_ref):
    tm = h_ref.shape[0] // SLAB
    n_used = tile_ref[0]

    @pl.when(pl.program_id(0) >= n_used)
    def _():
        y_ref[...] = jnp.zeros_like(y_ref)

    @pl.when(pl.program_id(0) < n_used)
    def _():
        _expert_tile(tile_ref[1 + pl.program_id(0)], tm, h_ref, gm_ref, wr_ref, wg_ref, wu_ref, wd_ref, gf_ref, y_ref)


def _expert_tile(grp, tm, h_ref, gm_ref, wr_ref, wg_ref, wu_ref, wd_ref, gf_ref, y_ref):
    h = _slabs_to_rows(h_ref, tm)
    hn_hi, hn_lo = _split_bf16(_rms(h, gm_ref[...]))
    part = _dot(hn_hi, wr_ref[...]) + _dot(hn_lo, wr_ref[...])
    gates = _route(part[:, :LANES] + part[:, LANES:], grp)
    lane = lax.broadcasted_iota(jnp.int32, gates.shape, 1)
    first = lax.broadcasted_iota(jnp.int32, (tm, 2 * D_EXPERT), 1) < D_EXPERT
    acts = []
    for j in range(EXPERTS_PER_GROUP // 2):
        e0 = N_GROUPS + grp * EXPERTS_PER_GROUP + 2 * j
        g0 = jnp.sum(jnp.where(lane == e0, gates, 0.0), axis=-1, keepdims=True)
        g1 = jnp.sum(jnp.where(lane == e0 + 1, gates, 0.0), axis=-1, keepdims=True)
        hg = _dot(hn_hi, wg_ref[j])
        hu = _dot(hn_hi, wu_ref[j])
        acts.append((jax.nn.silu(hg) * hu * jnp.where(first, g0, g1)).astype(BF16))
    y = _rms(h + _dot(jnp.concatenate(acts, axis=1), wd_ref[0]), gf_ref[...])
    _rows_to_slabs(y, y_ref)


def _moe_sorted(tiles, hs, g_moe, wr, wg2, wu2, wd2, g_final):
    blk = TOKEN_TILE * SLAB
    n_steps = hs.shape[0] // blk
    ppg = EXPERTS_PER_GROUP // 2
    used = lambda s, tl: jnp.minimum(s, tl[0] - 1)
    grp = lambda s, tl: tl[1 + used(s, tl)]
    const = lambda shape: pl.BlockSpec(shape, lambda s, tl: (0,) * len(shape))
    return pl.pallas_call(
        _moe_body,
        grid_spec=pltpu.PrefetchScalarGridSpec(
            num_scalar_prefetch=1, grid=(n_steps,),
            in_specs=[pl.BlockSpec((blk, LANES), lambda s, tl: (used(s, tl), 0)), const((1, D_MODEL)),
                      const(wr.shape),
                      pl.BlockSpec((ppg, D_MODEL, 2 * D_EXPERT), lambda s, tl: (grp(s, tl), 0, 0)),
                      pl.BlockSpec((ppg, D_MODEL, 2 * D_EXPERT), lambda s, tl: (grp(s, tl), 0, 0)),
                      pl.BlockSpec((1, ppg * 2 * D_EXPERT, D_MODEL), lambda s, tl: (grp(s, tl), 0, 0)),
                      const((1, D_MODEL))],
            out_specs=pl.BlockSpec((blk, LANES), lambda s, tl: (s, 0))),
        out_shape=jax.ShapeDtypeStruct(hs.shape, F32),
        compiler_params=_params("arbitrary"),
        name="moe",
    )(tiles, hs, g_moe, wr, wg2, wu2, wd2, g_final)


def _moe(h2, gid, cnt, w):
    n = h2.shape[0]
    tm = TOKEN_TILE
    n_steps = n // tm + N_GROUPS - 1
    cnt = cnt[:N_GROUPS, 0].astype(jnp.int32)
    padded = (cnt + tm - 1) // tm * tm
    end = jnp.cumsum(padded)
    start = end - padded
    tile_group = jnp.sum(jnp.arange(n_steps)[:, None] >= (end // tm)[None, :], axis=1)
    tiles = jnp.concatenate([end[-1:] // tm, jnp.minimum(tile_group, N_GROUPS - 1)]).astype(jnp.int32)
    fill = jnp.concatenate([start + cnt, end[-1:], padded - cnt, n_steps * tm - end[-1:]]).astype(jnp.int32)
    pos = _sorted_slots(start.astype(jnp.int32), gid, w["before"])
    hs = _scatter_rows(fill, pos, h2, n_steps * tm)
    ys = _moe_sorted(tiles, hs, w["g_moe"], w["wr"], w["wg2"], w["wu2"], w["wd2"], w["g_final"])
    return _gather_rows(pos, ys, n)


def _prep_weights(g_mix, w_in, g_sgu_v, sgu_w, sgu_b, na_rpb, g_out_a, g_out_b, w_out, g_xattn, g_mem,
                  w_xq, w_xkv, w_xo, g_moe, w_router_group, w_router_expert, w_exp_gate, w_exp_up,
                  w_exp_down, g_final):
    row = lambda g: g.reshape(1, -1).astype(F32)
    heads = sgu_w.shape[0]
    wcat = jnp.concatenate([sgu_w[0::2], sgu_w[1::2]], axis=2).astype(BF16)
    bexp = jnp.repeat(sgu_b.T, HEAD_DIM, axis=1).astype(F32)
    assert heads * HEAD_DIM == D_SGU
    wr = jnp.zeros((D_MODEL, LANES), F32)
    wr = wr.at[:, :N_GROUPS].set(w_router_group).at[:, N_GROUPS:N_GROUPS + N_EXPERTS].set(w_router_expert)
    wr_hi = wr.astype(BF16)
    wr_split = jnp.concatenate([wr_hi, (wr - wr_hi.astype(F32)).astype(BF16)], axis=1)
    before = jnp.triu(jnp.ones((TOKEN_TILE, TOKEN_TILE), BF16), k=1)
    pair_cols = lambda w: (w.reshape(N_EXPERTS // 2, 2, D_MODEL, D_EXPERT).transpose(0, 2, 1, 3)
                           .reshape(N_EXPERTS // 2, D_MODEL, 2 * D_EXPERT).astype(BF16))
    return dict(
        g_mix=row(g_mix), w_a=w_in[:, :2 * D_SGU].astype(BF16),
        w_qv=jnp.concatenate([w_in[:, 2 * D_SGU:2 * D_SGU + D_NA], w_in[:, 2 * D_SGU + 2 * D_NA:]], axis=1).astype(BF16),
        w_kt=w_in[:, 2 * D_SGU + D_NA:2 * D_SGU + 2 * D_NA].T.astype(BF16),
        g_sgu_v=row(g_sgu_v), wcat=wcat, bexp=bexp,
        bm=_na_bias_table(na_rpb), g_out_a=row(g_out_a), g_out_b=row(g_out_b), w_out=w_out.astype(BF16),
        g_xattn=row(g_xattn), g_mem=row(g_mem), w_xq=w_xq.astype(BF16), w_xkv=w_xkv.astype(BF16),
        w_xo=w_xo.astype(BF16), g_moe=row(g_moe), wr=wr_split, wrt=wr_split.T, before=before,
        wg2=pair_cols(w_exp_gate), wu2=pair_cols(w_exp_up),
        wd2=w_exp_down.reshape(N_GROUPS, EXPERTS_PER_GROUP * D_EXPERT, D_MODEL).astype(BF16),
        g_final=row(g_final))


def _trunk(x, mem, w):
    b, t, _ = x.shape
    km, vm = _mem_kv(mem, w["g_mem"], w["w_xkv"])
    q, kt, v, ya = _in_proj(x.reshape(b * t, D_MODEL), w["g_mix"], w["w_a"], w["w_qv"], w["w_kt"],
                            w["g_sgu_v"], w["wcat"], w["bexp"], w["g_out_a"])
    seq = lambda a: a.reshape(b, t, -1)
    yb = _na(seq(q), kt, seq(v), w["bm"])
    h2, gid, cnt = _mix_xattn(x, seq(ya), yb, km, vm, w["g_out_b"], w["w_out"], w["g_xattn"], w["w_xq"],
                              w["w_xo"], w["g_moe"], w["wrt"])
    y = _moe(h2.reshape(b * t, D_MODEL), gid, cnt, w)
    return y.reshape(b, t, D_MODEL)


def kernel(x_prompt, x_sample, mem_prompt, mem_sample, g_mix, w_in, g_sgu_v, sgu_w, sgu_b, na_rpb, g_out_a,
           g_out_b, w_out, g_xattn, g_mem, w_xq, w_xkv, w_xo, g_moe, w_router_group, w_router_expert,
           w_exp_gate, w_exp_up, w_exp_down, g_final):
    assert g_mix.shape[0] == 1
    w = _prep_weights(g_mix[0], w_in[0], g_sgu_v[0], sgu_w[0], sgu_b[0], na_rpb[0], g_out_a[0], g_out_b[0],
                      w_out[0], g_xattn[0], g_mem[0], w_xq[0], w_xkv[0], w_xo[0], g_moe[0],
                      w_router_group[0], w_router_expert[0], w_exp_gate[0], w_exp_up[0], w_exp_down[0],
                      g_final)
    return (_trunk(x_prompt, mem_prompt, w), _trunk(x_sample, mem_sample, w))
```
